```python
import math
import jax, jax.numpy as jnp
from jax import lax
import numpy as np

D_MODEL = 1024
BATCH = 4
SEQ = 4096
DEPTH = 2

N_MEM = 256
CHUNK = 128
A_GROUPS = 8
A_WIDTH = 1024
A_GROUP_DIM = A_WIDTH // A_GROUPS
B_HEADS = 10
B_WIDTH = 1280
B_HEAD_DIM = B_WIDTH // B_HEADS
CONV_WIDTH = 4
LRU_C = 8.0
C_HEADS = 4
C_HEAD_DIM = 256
C_WIDTH = C_HEADS * C_HEAD_DIM
N_BRANCH = 3
IN_WIDTH = 2 * A_WIDTH + 2 * B_WIDTH + C_WIDTH + N_BRANCH * D_MODEL
SPLIT_POINTS = (A_WIDTH, 2 * A_WIDTH, 2 * A_WIDTH + B_WIDTH, 2 * A_WIDTH + 2 * B_WIDTH,
                2 * A_WIDTH + 2 * B_WIDTH + C_WIDTH)
N_GROUPS = 4
EXPERTS_PER_GROUP = 4
N_EXPERTS = N_GROUPS * EXPERTS_PER_GROUP
TOP_K = 2
D_EXPERT = 256
ALPHA = (2 * DEPTH) ** 0.25
BETA = (8 * DEPTH) ** -0.25
LN_EPS = 1e-5

kernel_name = "hybrid_gmlp_rglru_memattn_hmoe_deepnorm"


def layer_norm(x, g, b):
    xf = x.astype(jnp.float32)
    mu = jnp.mean(xf, axis=-1, keepdims=True)
    var = jnp.mean(jnp.square(xf - mu), axis=-1, keepdims=True)
    y = (xf - mu) * lax.rsqrt(var + LN_EPS)
    return (y * g.astype(jnp.float32) + b.astype(jnp.float32)).astype(x.dtype)


def spatial_gating(u, v, ln_g, ln_b, w_s, b_s):
    bsz, seq, _ = v.shape
    n_chunks = seq // CHUNK
    v = layer_norm(v, ln_g, ln_b)
    causal = jnp.tril(jnp.ones((CHUNK, CHUNK), dtype=bool))
    w = jnp.where(causal[None], w_s, jnp.zeros_like(w_s))
    vc = v.reshape(bsz, n_chunks, CHUNK, A_GROUPS, A_GROUP_DIM)
    s = jnp.einsum('gts,bcsgd->bctgd', w, vc) + b_s.T[None, None, :, :, None]
    return u * s.reshape(bsz, seq, A_WIDTH)


def causal_depthwise_conv(x, w, b):
    seq = x.shape[1]
    xp = jnp.pad(x, ((0, 0), (CONV_WIDTH - 1, 0), (0, 0)))
    out = b
    for k in range(CONV_WIDTH):
        out = out + w[k] * xp[:, k:k + seq]
    return out


def rg_lru(x, w_a, b_a, w_x, b_x, lam):
    bsz, seq, _ = x.shape
    xh = x.reshape(bsz, seq, B_HEADS, B_HEAD_DIM)
    r = jax.nn.sigmoid(jnp.einsum('bshi,hij->bshj', xh, w_a).reshape(bsz, seq, B_WIDTH) + b_a)
    i = jax.nn.sigmoid(jnp.einsum('bshi,hij->bshj', xh, w_x).reshape(bsz, seq, B_WIDTH) + b_x)
    log_a = -LRU_C * r.astype(jnp.float32) * jax.nn.softplus(-lam.astype(jnp.float32))
    a = jnp.exp(log_a)
    mult = jnp.sqrt(-jnp.expm1(2.0 * log_a))
    bx = mult * (i * x).astype(jnp.float32)

    def combine(c1, c2):
        a1, b1 = c1
        a2, b2 = c2
        return a1 * a2, a2 * b1 + b2

    _, h = lax.associative_scan(combine, (a, bx), axis=1)
    return h.astype(x.dtype)


def memory_attention(q, mem_n, w_kv):
    bsz, seq, _ = q.shape
    kv = jnp.einsum('bmd,de->bme', mem_n, w_kv)
    k, v = jnp.split(kv, 2, axis=-1)
    qh = q.reshape(bsz, seq, C_HEADS, C_HEAD_DIM)
    kh = k.reshape(bsz, N_MEM, C_HEADS, C_HEAD_DIM)
    vh = v.reshape(bsz, N_MEM, C_HEADS, C_HEAD_DIM)
    scores = jnp.einsum('bshd,bmhd->bhsm', qh, kh).astype(jnp.float32) * (C_HEAD_DIM ** -0.5)
    p = jax.nn.softmax(scores, axis=-1).astype(q.dtype)
    o = jnp.einsum('bhsm,bmhd->bshd', p, vh)
    return o.reshape(bsz, seq, C_WIDTH)


def hybrid_mixer(x, mem_n, w_in, b_in, ln_v_g, ln_v_b, w_s, b_s, conv_w, conv_b,
                 w_a, b_a, w_x, b_x, lam, w_kv, p_a, p_b, p_c, w_o, b_o):
    z = jnp.einsum('bsd,de->bse', x, w_in) + b_in
    u_a, v_a, x_b, gate_b, q_c, merge = jnp.split(z, SPLIT_POINTS, axis=-1)
    o_a = spatial_gating(jax.nn.gelu(u_a), jax.nn.gelu(v_a), ln_v_g, ln_v_b, w_s, b_s)
    o_b = rg_lru(causal_depthwise_conv(x_b, conv_w, conv_b), w_a, b_a, w_x, b_x, lam) * jax.nn.gelu(gate_b)
    o_c = memory_attention(q_c, mem_n, w_kv)
    g_a, g_b, g_c = jnp.split(jax.nn.sigmoid(merge), N_BRANCH, axis=-1)
    y = (g_a * jnp.einsum('bse,ed->bsd', o_a, p_a)
         + g_b * jnp.einsum('bse,ed->bsd', o_b, p_b)
         + g_c * jnp.einsum('bse,ed->bsd', o_c, p_c))
    return jnp.einsum('bsd,de->bse', y, w_o) + b_o


def hierarchical_moe(x, w_rg, b_rg, w_re, b_re, w_up, w_down):
    bsz, seq, d = x.shape
    xf = x.reshape(bsz * seq, d)
    n_tok = xf.shape[0]
    group_p = jax.nn.softmax((xf @ w_rg + b_rg).astype(jnp.float32), axis=-1)
    p_top, g_top = lax.top_k(group_p, 1)
    exp_logits = (xf @ w_re + b_re).astype(jnp.float32).reshape(n_tok, N_GROUPS, EXPERTS_PER_GROUP)
    in_group = exp_logits[jnp.arange(n_tok), g_top[:, 0]]
    e_p, e_idx = lax.top_k(jax.nn.softmax(in_group, axis=-1), TOP_K)
    weights = p_top * e_p / jnp.sum(e_p, axis=-1, keepdims=True)
    expert_id = g_top * EXPERTS_PER_GROUP + e_idx
    gate = jnp.sum(jax.nn.one_hot(expert_id, N_EXPERTS, dtype=jnp.float32) * weights[..., None],
                   axis=1).astype(x.dtype)
    h = jnp.einsum('nd,edf->nef', xf, w_up)
    h_g, h_u = jnp.split(h, 2, axis=-1)
    h = jax.nn.silu(h_g) * h_u * gate[:, :, None]
    y = jnp.einsum('nef,efd->nd', h, w_down)
    return y.reshape(bsz, seq, d)


def setup_inputs(seed: int = 0) -> dict:
    key = jax.random.key(seed)
    ks = jax.random.split(key, 40)
    f32 = jnp.float32

    def nrm(k, shape, scale):
        return jax.random.normal(k, shape, dtype=f32) * scale

    lam_u = jax.random.uniform(ks[18], (DEPTH, B_WIDTH), dtype=f32, minval=0.9, maxval=0.999)
    a0 = lam_u ** (1.0 / LRU_C)
    lam = jnp.log(a0) - jnp.log1p(-a0)
    return {
        "x": nrm(ks[0], (BATCH, SEQ, D_MODEL), 1.0),
        "mem": nrm(ks[1], (BATCH, N_MEM, D_MODEL), 1.0),
        "ln_in_g": 1.0 + nrm(ks[2], (D_MODEL,), 0.02),
        "ln_in_b": nrm(ks[3], (D_MODEL,), 0.02),
        "ln_mem_g": 1.0 + nrm(ks[4], (D_MODEL,), 0.02),
        "ln_mem_b": nrm(ks[5], (D_MODEL,), 0.02),
        "w_in": nrm(ks[6], (DEPTH, D_MODEL, IN_WIDTH), D_MODEL ** -0.5),
        "b_in": nrm(ks[7], (DEPTH, IN_WIDTH), 0.02),
        "ln_v_g": 1.0 + nrm(ks[8], (DEPTH, A_WIDTH), 0.02),
        "ln_v_b": nrm(ks[9], (DEPTH, A_WIDTH), 0.02),
        "w_s": nrm(ks[10], (DEPTH, A_GROUPS, CHUNK, CHUNK), CHUNK ** -0.5),
        "b_s": 1.0 + nrm(ks[11], (DEPTH, A_GROUPS, CHUNK), 0.02),
        "conv_w": nrm(ks[12], (DEPTH, CONV_WIDTH, B_WIDTH), CONV_WIDTH ** -0.5),
        "conv_b": nrm(ks[13], (DEPTH, B_WIDTH), 0.02),
        "w_a": nrm(ks[14], (DEPTH, B_HEADS, B_HEAD_DIM, B_HEAD_DIM), B_HEAD_DIM ** -0.5),
        "b_a": nrm(ks[15], (DEPTH, B_WIDTH), 0.02),
        "w_x": nrm(ks[16], (DEPTH, B_HEADS, B_HEAD_DIM, B_HEAD_DIM), B_HEAD_DIM ** -0.5),
        "b_x": nrm(ks[17], (DEPTH, B_WIDTH), 0.02),
        "lam": lam,
        "w_kv": nrm(ks[19], (DEPTH, D_MODEL, 2 * C_WIDTH), D_MODEL ** -0.5),
        "p_a": nrm(ks[20], (DEPTH, A_WIDTH, D_MODEL), A_WIDTH ** -0.5),
        "p_b": nrm(ks[21], (DEPTH, B_WIDTH, D_MODEL), B_WIDTH ** -0.5),
        "p_c": nrm(ks[22], (DEPTH, C_WIDTH, D_MODEL), C_WIDTH ** -0.5),
        "w_o": nrm(ks[23], (DEPTH, D_MODEL, D_MODEL), BETA * D_MODEL ** -0.5),
        "b_o": nrm(ks[24], (DEPTH, D_MODEL), 0.02),
        "ln1_g": 1.0 + nrm(ks[25], (DEPTH, D_MODEL), 0.02),
        "ln1_b": nrm(ks[26], (DEPTH, D_MODEL), 0.02),
        "w_rg": nrm(ks[27], (DEPTH, D_MODEL, N_GROUPS), D_MODEL ** -0.5),
        "b_rg": nrm(ks[28], (DEPTH, N_GROUPS), 0.01),
        "w_re": nrm(ks[29], (DEPTH, D_MODEL, N_EXPERTS), D_MODEL ** -0.5),
        "b_re": nrm(ks[30], (DEPTH, N_EXPERTS), 0.01),
        "w_up": nrm(ks[31], (DEPTH, N_EXPERTS, D_MODEL, 2 * D_EXPERT), D_MODEL ** -0.5),
        "w_down": nrm(ks[32], (DEPTH, N_EXPERTS, D_EXPERT, D_MODEL), BETA * D_EXPERT ** -0.5),
        "ln2_g": 1.0 + nrm(ks[33], (DEPTH, D_MODEL), 0.02),
        "ln2_b": nrm(ks[34], (DEPTH, D_MODEL), 0.02),
    }


def reference(x, mem, ln_in_g, ln_in_b, ln_mem_g, ln_mem_b, w_in, b_in, ln_v_g, ln_v_b,
              w_s, b_s, conv_w, conv_b, w_a, b_a, w_x, b_x, lam, w_kv, p_a, p_b, p_c,
              w_o, b_o, ln1_g, ln1_b, w_rg, b_rg, w_re, b_re, w_up, w_down, ln2_g, ln2_b):
    x = layer_norm(x, ln_in_g, ln_in_b)
    mem_n = layer_norm(mem, ln_mem_g, ln_mem_b)
    for l in range(DEPTH):
        m = hybrid_mixer(x, mem_n, w_in[l], b_in[l], ln_v_g[l], ln_v_b[l], w_s[l], b_s[l],
                         conv_w[l], conv_b[l], w_a[l], b_a[l], w_x[l], b_x[l], lam[l],
                         w_kv[l], p_a[l], p_b[l], p_c[l], w_o[l], b_o[l])
        x = layer_norm(ALPHA * x + m, ln1_g[l], ln1_b[l])
        f = hierarchical_moe(x, w_rg[l], b_rg[l], w_re[l], b_re[l], w_up[l], w_down[l])
        x = layer_norm(ALPHA * x + f, ln2_g[l], ln2_b[l])
    return x
```

```python
import functools
import math

import jax
import jax.numpy as jnp
from jax import lax
from jax.experimental import pallas as pl
from jax.experimental.pallas import tpu as pltpu

D_MODEL = 1024
DEPTH = 2
N_MEM = 256
CHUNK = 128
A_GROUPS = 8
A_WIDTH = 1024
B_HEADS = 10
B_WIDTH = 1280
B_HEAD_DIM = B_WIDTH // B_HEADS
CONV_WIDTH = 4
LRU_C = 8.0
C_HEADS = 4
C_HEAD_DIM = 256
C_WIDTH = C_HEADS * C_HEAD_DIM
IN_WIDTH = 2 * A_WIDTH + 2 * B_WIDTH + C_WIDTH + 3 * D_MODEL
OFF_U = 0
OFF_V = A_WIDTH
OFF_XB = 2 * A_WIDTH
OFF_GB = OFF_XB + B_WIDTH
OFF_Q = OFF_GB + B_WIDTH
OFF_M = OFF_Q + C_WIDTH
N_GROUPS = 4
EXPERTS_PER_GROUP = 4
N_EXPERTS = 16
D_EXPERT = 256
ALPHA = (2 * DEPTH) ** 0.25
LN_EPS = 1e-5

SUBLANES = 8
LANES = 128
ROUTER_LANES = LANES
TOKEN_TILE = 256
VMEM_LIMIT = 56 * 1024 * 1024

BF16 = jnp.bfloat16
F32 = jnp.float32


def _dot(a, b):
    return jnp.dot(a, b, preferred_element_type=F32)


def _layer_norm(x, g, b):
    mu = jnp.mean(x, axis=-1, keepdims=True)
    xc = x - mu
    var = jnp.mean(xc * xc, axis=-1, keepdims=True)
    return xc * lax.rsqrt(var + LN_EPS) * g + b


def _gelu(x):
    c = math.sqrt(2.0 / math.pi)
    return 0.5 * x * (1.0 + jnp.tanh(c * (x + 0.044715 * (x * x * x))))


def _sigmoid(x):
    return 0.5 * (1.0 + jnp.tanh(0.5 * x))


def _prep_kernel(mem_ref, g_ref, b_ref, wkv_ref, kt_ref, v_ref):
    mem_n = _layer_norm(mem_ref[...], g_ref[...], b_ref[...]).astype(BF16)
    kv = _dot(mem_n, wkv_ref[...])
    kt_ref[...] = kv[:, :C_WIDTH].T.astype(BF16)
    v_ref[...] = kv[:, C_WIDTH:].astype(BF16)


def _prep_call(mem, ln_g, ln_b, w_kv_bf):
    bsz = mem.shape[0]
    return pl.pallas_call(
        _prep_kernel,
        grid=(DEPTH, bsz),
        in_specs=[
            pl.BlockSpec((None, N_MEM, D_MODEL), lambda l, b: (b, 0, 0)),
            pl.BlockSpec((1, D_MODEL), lambda l, b: (0, 0)),
            pl.BlockSpec((1, D_MODEL), lambda l, b: (0, 0)),
            pl.BlockSpec((None, D_MODEL, 2 * C_WIDTH), lambda l, b: (l, 0, 0)),
        ],
        out_specs=[
            pl.BlockSpec((None, None, C_WIDTH, N_MEM), lambda l, b: (l, b, 0, 0)),
            pl.BlockSpec((None, None, N_MEM, C_WIDTH), lambda l, b: (l, b, 0, 0)),
        ],
        out_shape=[
            jax.ShapeDtypeStruct((DEPTH, bsz, C_WIDTH, N_MEM), BF16),
            jax.ShapeDtypeStruct((DEPTH, bsz, N_MEM, C_WIDTH), BF16),
        ],
        compiler_params=pltpu.CompilerParams(
            dimension_semantics=("arbitrary", "arbitrary"), vmem_limit_bytes=VMEM_LIMIT),
    )(mem, ln_g.reshape(1, D_MODEL), ln_b.reshape(1, D_MODEL), w_kv_bf)


def _mixer_kernel(pre_ln, x_ref, lni_g_ref, lni_b_ref, kt_ref, vm_ref, w_in_ref, b_in_ref,
                  lnv_g_ref, lnv_b_ref, ws_ref, bs_ref, convw_ref, convb_ref, wax_ref, bax_ref,
                  lam_ref, pa_ref, pb_ref, pc_ref, wo_ref, bo_ref, ln1_g_ref, ln1_b_ref,
                  out_ref, conv_scr, a_scr, b_scr, h_scr, hlast_scr):
    tm = x_ref.shape[0]

    @pl.when(pl.program_id(1) == 0)
    def _():
        conv_scr[0:SUBLANES, :] = jnp.zeros((SUBLANES, B_WIDTH), F32)
        hlast_scr[...] = jnp.zeros((1, B_WIDTH), F32)

    x = x_ref[...]
    if pre_ln:
        x = _layer_norm(x, lni_g_ref[...], lni_b_ref[...])
    xb = x.astype(BF16)

    def proj(lo, width):
        return _dot(xb, w_in_ref[:, lo:lo + width]) + b_in_ref[:, lo:lo + width]

    u = _gelu(proj(OFF_U, A_WIDTH))
    vn = _layer_norm(_gelu(proj(OFF_V, A_WIDTH)), lnv_g_ref[...], lnv_b_ref[...]).astype(BF16)
    row = lax.broadcasted_iota(jnp.int32, (CHUNK, CHUNK), 0)
    col = lax.broadcasted_iota(jnp.int32, (CHUNK, CHUNK), 1)
    causal = col <= row
    ws = [jnp.where(causal, ws_ref[g], jnp.zeros((CHUNK, CHUNK), BF16)) for g in range(A_GROUPS)]
    s_chunks = []
    for c in range(tm // CHUNK):
        vc = vn[c * CHUNK:(c + 1) * CHUNK, :]
        gd = A_WIDTH // A_GROUPS
        s_c = jnp.concatenate(
            [_dot(ws[g], vc[:, g * gd:(g + 1) * gd]) for g in range(A_GROUPS)], axis=1)
        s_chunks.append(s_c + bs_ref[...])
    o_a = (u * jnp.concatenate(s_chunks, axis=0)).astype(BF16)
    y_a = _dot(o_a, pa_ref[...])

    xbr = proj(OFF_XB, B_WIDTH)
    conv_scr[SUBLANES:SUBLANES + tm, :] = xbr
    conv = convb_ref[...] + convw_ref[CONV_WIDTH - 1:CONV_WIDTH, :] * xbr
    for k in range(1, CONV_WIDTH):
        conv = conv + (convw_ref[CONV_WIDTH - 1 - k:CONV_WIDTH - k, :]
                       * conv_scr[SUBLANES - k:SUBLANES - k + tm, :])
    conv_scr[0:SUBLANES, :] = conv_scr[tm:tm + SUBLANES, :]
    cb = conv.astype(BF16)
    ri = jnp.concatenate(
        [_dot(cb[:, h * B_HEAD_DIM:(h + 1) * B_HEAD_DIM], wax_ref[h]) for h in range(B_HEADS)],
        axis=1)
    r_parts = [ri[:, (2 * h) * B_HEAD_DIM:(2 * h + 1) * B_HEAD_DIM] for h in range(B_HEADS)]
    i_parts = [ri[:, (2 * h + 1) * B_HEAD_DIM:(2 * h + 2) * B_HEAD_DIM] for h in range(B_HEADS)]
    r = _sigmoid(jnp.concatenate(r_parts, axis=1) + bax_ref[0:1, :])
    i = _sigmoid(jnp.concatenate(i_parts, axis=1) + bax_ref[1:2, :])
    z = -lam_ref[...]
    softplus = jnp.maximum(z, 0.0) + jnp.log(1.0 + jnp.exp(-jnp.abs(z)))
    log_a = (-LRU_C * softplus) * r
    a = jnp.exp(log_a)
    a_scr[...] = a
    b_scr[...] = jnp.sqrt(1.0 - a * a) * (i * conv)

    sub = lax.broadcasted_iota(jnp.int32, (SUBLANES, B_WIDTH), 0)
    masks = [sub >= d for d in (1, 2, 4)]

    def scan_body(j, hprev):
        r0 = pl.multiple_of(j * SUBLANES, SUBLANES)
        av = a_scr[pl.ds(r0, SUBLANES), :]
        bv = b_scr[pl.ds(r0, SUBLANES), :]
        for d, m in zip((1, 2, 4), masks):
            a_sh = jnp.where(m, pltpu.roll(av, d, 0), 1.0)
            b_sh = jnp.where(m, pltpu.roll(bv, d, 0), 0.0)
            bv = av * b_sh + bv
            av = av * a_sh
        h = av * hprev + bv
        h_scr[pl.ds(r0, SUBLANES), :] = h
        return h[SUBLANES - 1:SUBLANES, :]

    hlast_scr[...] = lax.fori_loop(0, tm // SUBLANES, scan_body, hlast_scr[...], unroll=2)
    o_b = (h_scr[...] * _gelu(proj(OFF_GB, B_WIDTH))).astype(BF16)
    y_b = _dot(o_b, pb_ref[...])

    q = proj(OFF_Q, C_WIDTH).astype(BF16)
    o_heads = []
    for h in range(C_HEADS):
        sl = slice(h * C_HEAD_DIM, (h + 1) * C_HEAD_DIM)
        s = _dot(q[:, sl], kt_ref[sl, :]) * (C_HEAD_DIM ** -0.5)
        p = jnp.exp(s - jnp.max(s, axis=-1, keepdims=True))
        denom = jnp.sum(p, axis=-1, keepdims=True)
        o_heads.append(_dot(p.astype(BF16), vm_ref[:, sl]) / denom)
    o_c = jnp.concatenate(o_heads, axis=1).astype(BF16)
    y_c = _dot(o_c, pc_ref[...])

    y = (_sigmoid(proj(OFF_M, D_MODEL)) * y_a
         + _sigmoid(proj(OFF_M + D_MODEL, D_MODEL)) * y_b
         + _sigmoid(proj(OFF_M + 2 * D_MODEL, D_MODEL)) * y_c)
    m = _dot(y.astype(BF16), wo_ref[...]) + bo_ref[...]
    out_ref[...] = _layer_norm(ALPHA * x + m, ln1_g_ref[...], ln1_b_ref[...])


def _resident(block_shape, index_map):
    return pl.BlockSpec(block_shape, index_map, pipeline_mode=pl.Buffered(1))


def _mixer_call(layer, pre_ln, x, lni_g, lni_b, kt, vm, w):
    bsz, seq, _ = x.shape
    tm = TOKEN_TILE

    def lw(*shape):
        return _resident((None,) + shape, lambda b, s: (layer,) + (0,) * len(shape))

    in_specs = [
        pl.BlockSpec((None, tm, D_MODEL), lambda b, s: (b, s, 0)),
        _resident((1, D_MODEL), lambda b, s: (0, 0)),
        _resident((1, D_MODEL), lambda b, s: (0, 0)),
        pl.BlockSpec((None, None, C_WIDTH, N_MEM), lambda b, s: (layer, b, 0, 0)),
        pl.BlockSpec((None, None, N_MEM, C_WIDTH), lambda b, s: (layer, b, 0, 0)),
        lw(D_MODEL, IN_WIDTH), lw(1, IN_WIDTH),
        lw(1, A_WIDTH), lw(1, A_WIDTH),
        lw(A_GROUPS, CHUNK, CHUNK), lw(CHUNK, A_WIDTH),
        lw(CONV_WIDTH, B_WIDTH), lw(1, B_WIDTH),
        lw(B_HEADS, B_HEAD_DIM, 2 * B_HEAD_DIM), lw(2, B_WIDTH), lw(1, B_WIDTH),
        lw(A_WIDTH, D_MODEL), lw(B_WIDTH, D_MODEL), lw(C_WIDTH, D_MODEL),
        lw(D_MODEL, D_MODEL), lw(1, D_MODEL), lw(1, D_MODEL), lw(1, D_MODEL),
    ]
    return pl.pallas_call(
        functools.partial(_mixer_kernel, pre_ln),
        grid=(bsz, seq // tm),
        in_specs=in_specs,
        out_specs=pl.BlockSpec((None, tm, D_MODEL), lambda b, s: (b, s, 0)),
        out_shape=jax.ShapeDtypeStruct(x.shape, F32),
        scratch_shapes=[
            pltpu.VMEM((tm + SUBLANES, B_WIDTH), F32),
            pltpu.VMEM((tm, B_WIDTH), F32),
            pltpu.VMEM((tm, B_WIDTH), F32),
            pltpu.VMEM((tm, B_WIDTH), F32),
            pltpu.VMEM((1, B_WIDTH), F32),
        ],
        compiler_params=pltpu.CompilerParams(
            dimension_semantics=("arbitrary", "arbitrary"), vmem_limit_bytes=VMEM_LIMIT),
    )(x, lni_g, lni_b, kt, vm, w["w_in"], w["b_in"], w["ln_v_g"], w["ln_v_b"], w["w_s"], w["b_s"],
      w["conv_w"], w["conv_b"], w["w_ax"], w["b_ax"], w["lam"], w["p_a"], w["p_b"], w["p_c"],
      w["w_o"], w["b_o"], w["ln1_g"], w["ln1_b"])


def _route(logits):
    lane = lax.broadcasted_iota(jnp.int32, logits.shape, 1)
    neg = -jnp.inf
    gl = jnp.where(lane < N_GROUPS, logits, neg)
    gmax = jnp.max(gl, axis=-1, keepdims=True)
    p_top = 1.0 / jnp.sum(jnp.exp(gl - gmax), axis=-1, keepdims=True)
    g_top = jnp.min(jnp.where(gl == gmax, lane, ROUTER_LANES), axis=-1, keepdims=True)
    e_lo = N_GROUPS + EXPERTS_PER_GROUP * g_top
    el = jnp.where((lane >= e_lo) & (lane < e_lo + EXPERTS_PER_GROUP), logits, neg)
    m1 = jnp.max(el, axis=-1, keepdims=True)
    i1 = jnp.min(jnp.where(el == m1, lane, ROUTER_LANES), axis=-1, keepdims=True)
    el2 = jnp.where(lane == i1, neg, el)
    m2 = jnp.max(el2, axis=-1, keepdims=True)
    i2 = jnp.min(jnp.where(el2 == m2, lane, ROUTER_LANES), axis=-1, keepdims=True)
    e2 = jnp.exp(m2 - m1)
    w1 = p_top / (1.0 + e2)
    w2 = w1 * e2
    return jnp.where(lane == i1, w1, 0.0) + jnp.where(lane == i2, w2, 0.0)


def _moe_kernel(x_ref, wr_hi_ref, wr_lo_ref, br_ref, wup_ref, wdn_ref, ln2_g_ref, ln2_b_ref, out_ref):
    x = x_ref[...]
    x_hi = x.astype(BF16)
    x_lo = (x - x_hi.astype(F32)).astype(BF16)
    logits = (_dot(x_hi, wr_hi_ref[...]) + _dot(x_lo, wr_hi_ref[...]) + _dot(x_hi, wr_lo_ref[...])
              + br_ref[...])
    gate = _route(logits)
    acts = []
    for e in range(N_EXPERTS):
        h = _dot(x_hi, wup_ref[e])
        hg = h[:, :D_EXPERT]
        act = hg * _sigmoid(hg) * h[:, D_EXPERT:] * gate[:, N_GROUPS + e:N_GROUPS + e + 1]
        acts.append(act.astype(BF16))
    f = _dot(jnp.concatenate(acts, axis=1), wdn_ref[...])
    out_ref[...] = _layer_norm(ALPHA * x + f, ln2_g_ref[...], ln2_b_ref[...])


def _moe_call(layer, x2d, w):
    n_tok = x2d.shape[0]
    tm = TOKEN_TILE

    def lw(*shape):
        return _resident((None,) + shape, lambda t: (layer,) + (0,) * len(shape))

    return pl.pallas_call(
        _moe_kernel,
        grid=(n_tok // tm,),
        in_specs=[
            pl.BlockSpec((tm, D_MODEL), lambda t: (t, 0)),
            lw(D_MODEL, ROUTER_LANES), lw(D_MODEL, ROUTER_LANES), lw(1, ROUTER_LANES),
            lw(N_EXPERTS, D_MODEL, 2 * D_EXPERT), lw(N_EXPERTS * D_EXPERT, D_MODEL),
            lw(1, D_MODEL), lw(1, D_MODEL),
        ],
        out_specs=pl.BlockSpec((tm, D_MODEL), lambda t: (t, 0)),
        out_shape=jax.ShapeDtypeStruct(x2d.shape, F32),
        compiler_params=pltpu.CompilerParams(
            dimension_semantics=("arbitrary",), vmem_limit_bytes=VMEM_LIMIT),
    )(x2d, w["wr_hi"], w["wr_lo"], w["br"], w["w_up"], w["w_down"], w["ln2_g"], w["ln2_b"])


def kernel(x, mem, ln_in_g, ln_in_b, ln_mem_g, ln_mem_b, w_in, b_in, ln_v_g, ln_v_b, w_s, b_s, conv_w, conv_b, w_a, b_a, w_x, b_x, lam, w_kv, p_a, p_b, p_c, w_o, b_o, ln1_g, ln1_b, w_rg, b_rg, w_re, b_re, w_up, w_down, ln2_g, ln2_b):
    bsz, seq, _ = x.shape
    assert seq % TOKEN_TILE == 0 and TOKEN_TILE % CHUNK == 0

    def row(p):
        return p[:, None, :]

    wr = jnp.concatenate(
        [w_rg, w_re, jnp.zeros((DEPTH, D_MODEL, ROUTER_LANES - N_GROUPS - N_EXPERTS), F32)], axis=-1)
    wr_hi = wr.astype(BF16)
    w = {
        "w_in": w_in.astype(BF16), "b_in": row(b_in),
        "ln_v_g": row(ln_v_g), "ln_v_b": row(ln_v_b),
        "w_s": w_s.astype(BF16),
        "b_s": jnp.repeat(jnp.swapaxes(b_s, 1, 2), A_WIDTH // A_GROUPS, axis=-1),
        "conv_w": conv_w, "conv_b": row(conv_b),
        "w_ax": jnp.concatenate([w_a, w_x], axis=-1).astype(BF16),
        "b_ax": jnp.stack([b_a, b_x], axis=1), "lam": row(lam),
        "p_a": p_a.astype(BF16), "p_b": p_b.astype(BF16), "p_c": p_c.astype(BF16),
        "w_o": w_o.astype(BF16), "b_o": row(b_o), "ln1_g": row(ln1_g), "ln1_b": row(ln1_b),
        "wr_hi": wr_hi, "wr_lo": (wr - wr_hi.astype(F32)).astype(BF16),
        "br": row(jnp.concatenate(
            [b_rg, b_re, jnp.zeros((DEPTH, ROUTER_LANES - N_GROUPS - N_EXPERTS), F32)], axis=-1)),
        "w_up": w_up.astype(BF16),
        "w_down": w_down.astype(BF16).reshape(DEPTH, N_EXPERTS * D_EXPERT, D_MODEL),
        "ln2_g": row(ln2_g), "ln2_b": row(ln2_b),
    }
    lni_g = ln_in_g.reshape(1, D_MODEL)
    lni_b = ln_in_b.reshape(1, D_MODEL)

    kt, vm = _prep_call(mem, ln_mem_g, ln_mem_b, w_kv.astype(BF16))
    for layer in range(DEPTH):
        x = _mixer_call(layer, layer == 0, x, lni_g, lni_b, kt, vm, w)
        x = _moe_call(layer, x.reshape(bsz * seq, D_MODEL), w).reshape(bsz, seq, D_MODEL)
    return x
```

```python
import functools
import math

import jax
import jax.numpy as jnp
from jax import lax
from jax.experimental import pallas as pl
from jax.experimental.pallas import tpu as pltpu

D_MODEL = 1024
DEPTH = 2
N_MEM = 256
CHUNK = 128
A_GROUPS = 8
A_WIDTH = 1024
B_HEADS = 10
B_WIDTH = 1280
B_HEAD_DIM = B_WIDTH // B_HEADS
CONV_WIDTH = 4
LRU_C = 8.0
C_HEADS = 4
C_HEAD_DIM = 256
C_WIDTH = C_HEADS * C_HEAD_DIM
IN_WIDTH = 2 * A_WIDTH + 2 * B_WIDTH + C_WIDTH + 3 * D_MODEL
OFF_U = 0
OFF_V = A_WIDTH
OFF_XB = 2 * A_WIDTH
OFF_GB = OFF_XB + B_WIDTH
OFF_Q = OFF_GB + B_WIDTH
OFF_M = OFF_Q + C_WIDTH
N_GROUPS = 4
EXPERTS_PER_GROUP = 4
N_EXPERTS = 16
D_EXPERT = 256
ALPHA = (2 * DEPTH) ** 0.25
LN_EPS = 1e-5

SUBLANES = 8
LANES = 128
ROUTER_LANES = LANES
TOKEN_TILE = 256
MOE_TILE = 512
ROW_BLOCK = 128
MAX_BLOCKS = MOE_TILE // ROW_BLOCK + N_GROUPS
SORTED_ROWS = MAX_BLOCKS * ROW_BLOCK
VMEM_LIMIT = 56 * 1024 * 1024

BF16 = jnp.bfloat16
F32 = jnp.float32


def _dot(a, b):
    return jnp.dot(a, b, preferred_element_type=F32)


def _layer_norm(x, g, b):
    mu = jnp.mean(x, axis=-1, keepdims=True)
    xc = x - mu
    var = jnp.mean(xc * xc, axis=-1, keepdims=True)
    return xc * lax.rsqrt(var + LN_EPS) * g + b


def _gelu(x):
    c = math.sqrt(2.0 / math.pi)
    return 0.5 * x * (1.0 + jnp.tanh(c * (x + 0.044715 * (x * x * x))))


def _sigmoid(x):
    return 0.5 * (1.0 + jnp.tanh(0.5 * x))


def _prep_kernel(mem_ref, g_ref, b_ref, wkv_ref, kt_ref, v_ref):
    mem_n = _layer_norm(mem_ref[...], g_ref[...], b_ref[...]).astype(BF16)
    kv = _dot(mem_n, wkv_ref[...])
    kt_ref[...] = kv[:, :C_WIDTH].T.astype(BF16)
    v_ref[...] = kv[:, C_WIDTH:].astype(BF16)


def _prep_call(mem, ln_g, ln_b, w_kv_bf):
    bsz = mem.shape[0]
    return pl.pallas_call(
        _prep_kernel,
        grid=(DEPTH, bsz),
        in_specs=[
            pl.BlockSpec((None, N_MEM, D_MODEL), lambda l, b: (b, 0, 0)),
            pl.BlockSpec((1, D_MODEL), lambda l, b: (0, 0)),
            pl.BlockSpec((1, D_MODEL), lambda l, b: (0, 0)),
            pl.BlockSpec((None, D_MODEL, 2 * C_WIDTH), lambda l, b: (l, 0, 0)),
        ],
        out_specs=[
            pl.BlockSpec((None, None, C_WIDTH, N_MEM), lambda l, b: (l, b, 0, 0)),
            pl.BlockSpec((None, None, N_MEM, C_WIDTH), lambda l, b: (l, b, 0, 0)),
        ],
        out_shape=[
            jax.ShapeDtypeStruct((DEPTH, bsz, C_WIDTH, N_MEM), BF16),
            jax.ShapeDtypeStruct((DEPTH, bsz, N_MEM, C_WIDTH), BF16),
        ],
        compiler_params=pltpu.CompilerParams(
            dimension_semantics=("arbitrary", "arbitrary"), vmem_limit_bytes=VMEM_LIMIT),
    )(mem, ln_g.reshape(1, D_MODEL), ln_b.reshape(1, D_MODEL), w_kv_bf)


def _mixer_kernel(pre_ln, x_ref, lni_g_ref, lni_b_ref, kt_ref, vm_ref, w_in_ref, b_in_ref,
                  lnv_g_ref, lnv_b_ref, ws_ref, bs_ref, convw_ref, convb_ref, wax_ref, bax_ref,
                  lam_ref, pa_ref, pb_ref, pc_ref, wo_ref, bo_ref, ln1_g_ref, ln1_b_ref,
                  out_ref, conv_scr, a_scr, b_scr, h_scr, hlast_scr):
    tm = x_ref.shape[0]

    @pl.when(pl.program_id(1) == 0)
    def _():
        conv_scr[0:SUBLANES, :] = jnp.zeros((SUBLANES, B_WIDTH), F32)
        hlast_scr[...] = jnp.zeros((1, B_WIDTH), F32)

    x = x_ref[...]
    if pre_ln:
        x = _layer_norm(x, lni_g_ref[...], lni_b_ref[...])
    xb = x.astype(BF16)

    def proj(lo, width):
        return _dot(xb, w_in_ref[:, lo:lo + width]) + b_in_ref[:, lo:lo + width]

    u = _gelu(proj(OFF_U, A_WIDTH))
    vn = _layer_norm(_gelu(proj(OFF_V, A_WIDTH)), lnv_g_ref[...], lnv_b_ref[...]).astype(BF16)
    row = lax.broadcasted_iota(jnp.int32, (CHUNK, CHUNK), 0)
    col = lax.broadcasted_iota(jnp.int32, (CHUNK, CHUNK), 1)
    causal = col <= row
    ws = [jnp.where(causal, ws_ref[g], jnp.zeros((CHUNK, CHUNK), BF16)) for g in range(A_GROUPS)]
    s_chunks = []
    for c in range(tm // CHUNK):
        vc = vn[c * CHUNK:(c + 1) * CHUNK, :]
        gd = A_WIDTH // A_GROUPS
        s_c = jnp.concatenate(
            [_dot(ws[g], vc[:, g * gd:(g + 1) * gd]) for g in range(A_GROUPS)], axis=1)
        s_chunks.append(s_c + bs_ref[...])
    o_a = (u * jnp.concatenate(s_chunks, axis=0)).astype(BF16)
    y_a = _dot(o_a, pa_ref[...])

    xbr = proj(OFF_XB, B_WIDTH)
    conv_scr[SUBLANES:SUBLANES + tm, :] = xbr
    conv = convb_ref[...] + convw_ref[CONV_WIDTH - 1:CONV_WIDTH, :] * xbr
    for k in range(1, CONV_WIDTH):
        conv = conv + (convw_ref[CONV_WIDTH - 1 - k:CONV_WIDTH - k, :]
                       * conv_scr[SUBLANES - k:SUBLANES - k + tm, :])
    conv_scr[0:SUBLANES, :] = conv_scr[tm:tm + SUBLANES, :]
    cb = conv.astype(BF16)
    ri = jnp.concatenate(
        [_dot(cb[:, h * B_HEAD_DIM:(h + 1) * B_HEAD_DIM], wax_ref[h]) for h in range(B_HEADS)],
        axis=1)
    r_parts = [ri[:, (2 * h) * B_HEAD_DIM:(2 * h + 1) * B_HEAD_DIM] for h in range(B_HEADS)]
    i_parts = [ri[:, (2 * h + 1) * B_HEAD_DIM:(2 * h + 2) * B_HEAD_DIM] for h in range(B_HEADS)]
    r = _sigmoid(jnp.concatenate(r_parts, axis=1) + bax_ref[0:1, :])
    i = _sigmoid(jnp.concatenate(i_parts, axis=1) + bax_ref[1:2, :])
    z = -lam_ref[...]
    softplus = jnp.maximum(z, 0.0) + jnp.log(1.0 + jnp.exp(-jnp.abs(z)))
    log_a = (-LRU_C * softplus) * r
    a = jnp.exp(log_a)
    a_scr[...] = a
    b_scr[...] = jnp.sqrt(1.0 - a * a) * (i * conv)

    sub = lax.broadcasted_iota(jnp.int32, (SUBLANES, B_WIDTH), 0)
    masks = [sub >= d for d in (1, 2, 4)]

    def scan_body(j, hprev):
        r0 = pl.multiple_of(j * SUBLANES, SUBLANES)
        av = a_scr[pl.ds(r0, SUBLANES), :]
        bv = b_scr[pl.ds(r0, SUBLANES), :]
        for d, m in zip((1, 2, 4), masks):
            a_sh = jnp.where(m, pltpu.roll(av, d, 0), 1.0)
            b_sh = jnp.where(m, pltpu.roll(bv, d, 0), 0.0)
            bv = av * b_sh + bv
            av = av * a_sh
        h = av * hprev + bv
        h_scr[pl.ds(r0, SUBLANES), :] = h
        return h[SUBLANES - 1:SUBLANES, :]

    hlast_scr[...] = lax.fori_loop(0, tm // SUBLANES, scan_body, hlast_scr[...], unroll=2)
    o_b = (h_scr[...] * _gelu(proj(OFF_GB, B_WIDTH))).astype(BF16)
    y_b = _dot(o_b, pb_ref[...])

    q = proj(OFF_Q, C_WIDTH).astype(BF16)
    o_heads = []
    for h in range(C_HEADS):
        sl = slice(h * C_HEAD_DIM, (h + 1) * C_HEAD_DIM)
        s = _dot(q[:, sl], kt_ref[sl, :]) * (C_HEAD_DIM ** -0.5)
        p = jnp.exp(s - jnp.max(s, axis=-1, keepdims=True))
        denom = jnp.sum(p, axis=-1, keepdims=True)
        o_heads.append(_dot(p.astype(BF16), vm_ref[:, sl]) / denom)
    o_c = jnp.concatenate(o_heads, axis=1).astype(BF16)
    y_c = _dot(o_c, pc_ref[...])

    y = (_sigmoid(proj(OFF_M, D_MODEL)) * y_a
         + _sigmoid(proj(OFF_M + D_MODEL, D_MODEL)) * y_b
         + _sigmoid(proj(OFF_M + 2 * D_MODEL, D_MODEL)) * y_c)
    m = _dot(y.astype(BF16), wo_ref[...]) + bo_ref[...]
    out_ref[...] = _layer_norm(ALPHA * x + m, ln1_g_ref[...], ln1_b_ref[...])


def _resident(block_shape, index_map):
    return pl.BlockSpec(block_shape, index_map, pipeline_mode=pl.Buffered(1))


def _mixer_call(layer, pre_ln, x, lni_g, lni_b, kt, vm, w):
    bsz, seq, _ = x.shape
    tm = TOKEN_TILE

    def lw(*shape):
        return _resident((None,) + shape, lambda b, s: (layer,) + (0,) * len(shape))

    in_specs = [
        pl.BlockSpec((None, tm, D_MODEL), lambda b, s: (b, s, 0)),
        _resident((1, D_MODEL), lambda b, s: (0, 0)),
        _resident((1, D_MODEL), lambda b, s: (0, 0)),
        pl.BlockSpec((None, None, C_WIDTH, N_MEM), lambda b, s: (layer, b, 0, 0)),
        pl.BlockSpec((None, None, N_MEM, C_WIDTH), lambda b, s: (layer, b, 0, 0)),
        lw(D_MODEL, IN_WIDTH), lw(1, IN_WIDTH),
        lw(1, A_WIDTH), lw(1, A_WIDTH),
        lw(A_GROUPS, CHUNK, CHUNK), lw(CHUNK, A_WIDTH),
        lw(CONV_WIDTH, B_WIDTH), lw(1, B_WIDTH),
        lw(B_HEADS, B_HEAD_DIM, 2 * B_HEAD_DIM), lw(2, B_WIDTH), lw(1, B_WIDTH),
        lw(A_WIDTH, D_MODEL), lw(B_WIDTH, D_MODEL), lw(C_WIDTH, D_MODEL),
        lw(D_MODEL, D_MODEL), lw(1, D_MODEL), lw(1, D_MODEL), lw(1, D_MODEL),
    ]
    return pl.pallas_call(
        functools.partial(_mixer_kernel, pre_ln),
        grid=(bsz, seq // tm),
        in_specs=in_specs,
        out_specs=pl.BlockSpec((None, tm, D_MODEL), lambda b, s: (b, s, 0)),
        out_shape=jax.ShapeDtypeStruct(x.shape, F32),
        scratch_shapes=[
            pltpu.VMEM((tm + SUBLANES, B_WIDTH), F32),
            pltpu.VMEM((tm, B_WIDTH), F32),
            pltpu.VMEM((tm, B_WIDTH), F32),
            pltpu.VMEM((tm, B_WIDTH), F32),
            pltpu.VMEM((1, B_WIDTH), F32),
        ],
        compiler_params=pltpu.CompilerParams(
            dimension_semantics=("arbitrary", "arbitrary"), vmem_limit_bytes=VMEM_LIMIT),
    )(x, lni_g, lni_b, kt, vm, w["w_in"], w["b_in"], w["ln_v_g"], w["ln_v_b"], w["w_s"], w["b_s"],
      w["conv_w"], w["conv_b"], w["w_ax"], w["b_ax"], w["lam"], w["p_a"], w["p_b"], w["p_c"],
      w["w_o"], w["b_o"], w["ln1_g"], w["ln1_b"])


def _first_max_of4(c):
    m = jnp.maximum(jnp.maximum(c[0], c[1]), jnp.maximum(c[2], c[3]))
    idx = jnp.where(c[0] == m, 0, jnp.where(c[1] == m, 1, jnp.where(c[2] == m, 2, 3)))
    return m, idx


def _moe_kernel(x_ref, wr_hi_ref, wr_lo_ref, br_ref, wup_ref, wdn_ref, ln2_g_ref, ln2_b_ref, out_ref,
                xs_scr, gs_scr, f_scr):
    tm = x_ref.shape[0]
    x = x_ref[...]
    x_hi = x.astype(BF16)
    x_lo = (x - x_hi.astype(F32)).astype(BF16)
    logits = (_dot(x_hi, wr_hi_ref[...]) + _dot(x_lo, wr_hi_ref[...]) + _dot(x_hi, wr_lo_ref[...])
              + br_ref[...])
    lane = lax.broadcasted_iota(jnp.int32, logits.shape, 1)
    neg = -jnp.inf

    gl = jnp.where(lane < N_GROUPS, logits, neg)
    gmax = jnp.max(gl, axis=-1, keepdims=True)
    p_top = 1.0 / jnp.sum(jnp.exp(gl - gmax), axis=-1, keepdims=True)
    g_top = jnp.min(jnp.where(gl == gmax, lane, ROUTER_LANES), axis=-1, keepdims=True)
    in_group = lane == g_top

    e_lo = N_GROUPS + EXPERTS_PER_GROUP * g_top
    cand = [jnp.sum(jnp.where(lane == e_lo + j, logits, 0.0), axis=-1, keepdims=True)
            for j in range(EXPERTS_PER_GROUP)]
    m1, i1 = _first_max_of4(cand)
    m2, i2 = _first_max_of4([jnp.where(i1 == j, neg, cand[j]) for j in range(EXPERTS_PER_GROUP)])
    e2 = jnp.exp(m2 - m1)
    w1 = p_top / (1.0 + e2)
    w2 = w1 * e2
    gate4 = jnp.where(lane == i1, w1, 0.0) + jnp.where(lane == i2, w2, 0.0)

    onehot = jnp.where(in_group, 1.0, 0.0)
    tok_r = lax.broadcasted_iota(jnp.int32, (tm, tm), 0)
    tok_c = lax.broadcasted_iota(jnp.int32, (tm, tm), 1)
    earlier = jnp.where(tok_c < tok_r, 1.0, 0.0).astype(BF16)
    rank = _dot(earlier, onehot.astype(BF16))
    counts = jnp.sum(onehot, axis=0, keepdims=True).astype(jnp.int32)
    ends = []
    end = 0
    base_vec = jnp.zeros((1, ROUTER_LANES), F32)
    lane1 = lax.broadcasted_iota(jnp.int32, (1, ROUTER_LANES), 1)
    for g in range(N_GROUPS):
        if g:
            base_vec = jnp.where(lane1 == g, end.astype(F32), base_vec)
        n_blocks = lax.shift_right_logical(counts[0, g] + (ROW_BLOCK - 1), ROW_BLOCK.bit_length() - 1)
        end = end + n_blocks * ROW_BLOCK
        ends.append(end)
    dest = jnp.sum(jnp.where(in_group, rank + base_vec, 0.0), axis=-1, keepdims=True).astype(jnp.int32)

    dest_row = jnp.broadcast_to(dest, (tm, LANES)).T[0:1, :]
    perm = jnp.where(lax.broadcasted_iota(jnp.int32, (SORTED_ROWS, tm), 0) == dest_row,
                     1.0, 0.0).astype(BF16)
    perm_t = jnp.where(lax.broadcasted_iota(jnp.int32, (tm, SORTED_ROWS), 1) == dest,
                       1.0, 0.0).astype(BF16)
    xs_scr[...] = _dot(perm, x_hi).astype(BF16)
    g_hi = gate4.astype(BF16)
    g_lo = (gate4 - g_hi.astype(F32)).astype(BF16)
    gs2 = _dot(perm, jnp.concatenate([g_hi, g_lo], axis=1))
    gs_scr[...] = gs2[:, :ROUTER_LANES] + gs2[:, ROUTER_LANES:]

    for b in range(MAX_BLOCKS):
        start = b * ROW_BLOCK
        rows = slice(start, start + ROW_BLOCK)
        grp = sum((start >= ends[g]).astype(jnp.int32) for g in range(N_GROUPS - 1))
        used = start < ends[N_GROUPS - 1]

        @pl.when(used)
        def _():
            xsb = xs_scr[rows, :]
            gsb = gs_scr[rows, :]
            acts = []
            for j in range(EXPERTS_PER_GROUP):
                h = _dot(xsb, wup_ref[grp * EXPERTS_PER_GROUP + j])
                hg = h[:, :D_EXPERT]
                acts.append((hg * _sigmoid(hg) * h[:, D_EXPERT:] * gsb[:, j:j + 1]).astype(BF16))
            f_scr[rows, :] = _dot(jnp.concatenate(acts, axis=1), wdn_ref[grp]).astype(BF16)

        @pl.when(jnp.logical_not(used))
        def _():
            f_scr[rows, :] = jnp.zeros((ROW_BLOCK, D_MODEL), BF16)

    f = _dot(perm_t, f_scr[...])
    out_ref[...] = _layer_norm(ALPHA * x + f, ln2_g_ref[...], ln2_b_ref[...])


def _moe_call(layer, x2d, w):
    n_tok = x2d.shape[0]
    tm = MOE_TILE

    def lw(*shape):
        return _resident((None,) + shape, lambda t: (layer,) + (0,) * len(shape))

    return pl.pallas_call(
        _moe_kernel,
        grid=(n_tok // tm,),
        in_specs=[
            pl.BlockSpec((tm, D_MODEL), lambda t: (t, 0)),
            lw(D_MODEL, ROUTER_LANES), lw(D_MODEL, ROUTER_LANES), lw(1, ROUTER_LANES),
            lw(N_EXPERTS, D_MODEL, 2 * D_EXPERT), lw(N_GROUPS, EXPERTS_PER_GROUP * D_EXPERT, D_MODEL),
            lw(1, D_MODEL), lw(1, D_MODEL),
        ],
        out_specs=pl.BlockSpec((tm, D_MODEL), lambda t: (t, 0)),
        out_shape=jax.ShapeDtypeStruct(x2d.shape, F32),
        scratch_shapes=[
            pltpu.VMEM((SORTED_ROWS, D_MODEL), BF16),
            pltpu.VMEM((SORTED_ROWS, ROUTER_LANES), F32),
            pltpu.VMEM((SORTED_ROWS, D_MODEL), BF16),
        ],
        compiler_params=pltpu.CompilerParams(
            dimension_semantics=("arbitrary",), vmem_limit_bytes=VMEM_LIMIT),
    )(x2d, w["wr_hi"], w["wr_lo"], w["br"], w["w_up"], w["w_down"], w["ln2_g"], w["ln2_b"])


def kernel(x, mem, ln_in_g, ln_in_b, ln_mem_g, ln_mem_b, w_in, b_in, ln_v_g, ln_v_b, w_s, b_s, conv_w, conv_b, w_a, b_a, w_x, b_x, lam, w_kv, p_a, p_b, p_c, w_o, b_o, ln1_g, ln1_b, w_rg, b_rg, w_re, b_re, w_up, w_down, ln2_g, ln2_b):
    bsz, seq, _ = x.shape
    assert seq % TOKEN_TILE == 0 and TOKEN_TILE % CHUNK == 0 and (bsz * seq) % MOE_TILE == 0

    def row(p):
        return p[:, None, :]

    wr = jnp.concatenate(
        [w_rg, w_re, jnp.zeros((DEPTH, D_MODEL, ROUTER_LANES - N_GROUPS - N_EXPERTS), F32)], axis=-1)
    wr_hi = wr.astype(BF16)
    w = {
        "w_in": w_in.astype(BF16), "b_in": row(b_in),
        "ln_v_g": row(ln_v_g), "ln_v_b": row(ln_v_b),
        "w_s": w_s.astype(BF16),
        "b_s": jnp.repeat(jnp.swapaxes(b_s, 1, 2), A_WIDTH // A_GROUPS, axis=-1),
        "conv_w": conv_w, "conv_b": row(conv_b),
        "w_ax": jnp.concatenate([w_a, w_x], axis=-1).astype(BF16),
        "b_ax": jnp.stack([b_a, b_x], axis=1), "lam": row(lam),
        "p_a": p_a.astype(BF16), "p_b": p_b.astype(BF16), "p_c": p_c.astype(BF16),
        "w_o": w_o.astype(BF16), "b_o": row(b_o), "ln1_g": row(ln1_g), "ln1_b": row(ln1_b),
        "wr_hi": wr_hi, "wr_lo": (wr - wr_hi.astype(F32)).astype(BF16),
        "br": row(jnp.concatenate(
            [b_rg, b_re, jnp.zeros((DEPTH, ROUTER_LANES - N_GROUPS - N_EXPERTS), F32)], axis=-1)),
        "w_up": w_up.astype(BF16),
        "w_down": w_down.astype(BF16).reshape(DEPTH, N_GROUPS, EXPERTS_PER_GROUP * D_EXPERT, D_MODEL),
        "ln2_g": row(ln2_g), "ln2_b": row(ln2_b),
    }
    lni_g = ln_in_g.reshape(1, D_MODEL)
    lni_b = ln_in_b.reshape(1, D_MODEL)

    kt, vm = _prep_call(mem, ln_mem_g, ln_mem_b, w_kv.astype(BF16))
    for layer in range(DEPTH):
        x = _mixer_call(layer, layer == 0, x, lni_g, lni_b, kt, vm, w)
        x = _moe_call(layer, x.reshape(bsz * seq, D_MODEL), w).reshape(bsz, seq, D_MODEL)
    return x
```

```python
import functools
import math

import jax
import jax.numpy as jnp
from jax import lax
from jax.experimental import pallas as pl
from jax.experimental.pallas import tpu as pltpu

D_MODEL = 1024
DEPTH = 2
N_MEM = 256
CHUNK = 128
A_GROUPS = 8
A_WIDTH = 1024
B_HEADS = 10
B_WIDTH = 1280
B_HEAD_DIM = B_WIDTH // B_HEADS
CONV_WIDTH = 4
LRU_C = 8.0
C_HEADS = 4
C_HEAD_DIM = 256
C_WIDTH = C_HEADS * C_HEAD_DIM
IN_WIDTH = 2 * A_WIDTH + 2 * B_WIDTH + C_WIDTH + 3 * D_MODEL
OFF_U = 0
OFF_V = A_WIDTH
OFF_XB = 2 * A_WIDTH
OFF_GB = OFF_XB + B_WIDTH
OFF_Q = OFF_GB + B_WIDTH
OFF_M = OFF_Q + C_WIDTH
N_GROUPS = 4
EXPERTS_PER_GROUP = 4
N_EXPERTS = 16
D_EXPERT = 256
ALPHA = (2 * DEPTH) ** 0.25
LN_EPS = 1e-5

SUBLANES = 8
LANES = 128
ROUTER_LANES = LANES
TOKEN_TILE = 256
MOE_TILE = 512
ROW_BLOCK = 128
MAX_BLOCKS = MOE_TILE // ROW_BLOCK + N_GROUPS
SORTED_ROWS = MAX_BLOCKS * ROW_BLOCK
UNSORT_CHUNK = 2 * ROW_BLOCK
VMEM_LIMIT = 56 * 1024 * 1024

BF16 = jnp.bfloat16
F32 = jnp.float32


def _dot(a, b):
    return jnp.dot(a, b, preferred_element_type=F32)


def _layer_norm(x, g, b):
    mu = jnp.mean(x, axis=-1, keepdims=True)
    xc = x - mu
    var = jnp.mean(xc * xc, axis=-1, keepdims=True)
    return xc * lax.rsqrt(var + LN_EPS) * g + b


def _gelu(x):
    c = math.sqrt(2.0 / math.pi)
    return 0.5 * x * (1.0 + jnp.tanh(c * (x + 0.044715 * (x * x * x))))


def _sigmoid(x):
    return 0.5 * (1.0 + jnp.tanh(0.5 * x))


def _prep_kernel(mem_ref, g_ref, b_ref, wkv_ref, kt_ref, v_ref):
    mem_n = _layer_norm(mem_ref[...], g_ref[...], b_ref[...]).astype(BF16)
    kv = _dot(mem_n, wkv_ref[...])
    kt_ref[...] = kv[:, :C_WIDTH].T.astype(BF16)
    v_ref[...] = kv[:, C_WIDTH:].astype(BF16)


def _prep_call(mem, ln_g, ln_b, w_kv_bf):
    bsz = mem.shape[0]
    return pl.pallas_call(
        _prep_kernel,
        grid=(DEPTH, bsz),
        in_specs=[
            pl.BlockSpec((None, N_MEM, D_MODEL), lambda l, b: (b, 0, 0)),
            pl.BlockSpec((1, D_MODEL), lambda l, b: (0, 0)),
            pl.BlockSpec((1, D_MODEL), lambda l, b: (0, 0)),
            pl.BlockSpec((None, D_MODEL, 2 * C_WIDTH), lambda l, b: (l, 0, 0)),
        ],
        out_specs=[
            pl.BlockSpec((None, None, C_WIDTH, N_MEM), lambda l, b: (l, b, 0, 0)),
            pl.BlockSpec((None, None, N_MEM, C_WIDTH), lambda l, b: (l, b, 0, 0)),
        ],
        out_shape=[
            jax.ShapeDtypeStruct((DEPTH, bsz, C_WIDTH, N_MEM), BF16),
            jax.ShapeDtypeStruct((DEPTH, bsz, N_MEM, C_WIDTH), BF16),
        ],
        compiler_params=pltpu.CompilerParams(
            dimension_semantics=("arbitrary", "arbitrary"), vmem_limit_bytes=VMEM_LIMIT),
    )(mem, ln_g.reshape(1, D_MODEL), ln_b.reshape(1, D_MODEL), w_kv_bf)


def _mixer_kernel(pre_ln, x_ref, lni_g_ref, lni_b_ref, kt_ref, vm_ref, w_in_ref, b_in_ref,
                  lnv_g_ref, lnv_b_ref, ws_ref, bs_ref, convw_ref, convb_ref, wax_ref, bax_ref,
                  lam_ref, pa_ref, pb_ref, pc_ref, wo_ref, bo_ref, ln1_g_ref, ln1_b_ref,
                  out_ref, seg_scr, tail_scr, hlast_scr):
    tm = x_ref.shape[0]

    @pl.when(pl.program_id(1) == 0)
    def _():
        tail_scr[...] = jnp.zeros(tail_scr.shape, F32)
        hlast_scr[...] = jnp.zeros((1, B_WIDTH), F32)

    x = x_ref[...]
    if pre_ln:
        x = _layer_norm(x, lni_g_ref[...], lni_b_ref[...])
    xb = x.astype(BF16)

    def proj(lo, width):
        return _dot(xb, w_in_ref[:, lo:lo + width]) + b_in_ref[:, lo:lo + width]

    u = _gelu(proj(OFF_U, A_WIDTH))
    vn = _layer_norm(_gelu(proj(OFF_V, A_WIDTH)), lnv_g_ref[...], lnv_b_ref[...]).astype(BF16)
    row = lax.broadcasted_iota(jnp.int32, (CHUNK, CHUNK), 0)
    col = lax.broadcasted_iota(jnp.int32, (CHUNK, CHUNK), 1)
    causal = col <= row
    ws = [jnp.where(causal, ws_ref[g], jnp.zeros((CHUNK, CHUNK), BF16)) for g in range(A_GROUPS)]
    s_chunks = []
    for c in range(tm // CHUNK):
        vc = vn[c * CHUNK:(c + 1) * CHUNK, :]
        gd = A_WIDTH // A_GROUPS
        s_c = jnp.concatenate(
            [_dot(ws[g], vc[:, g * gd:(g + 1) * gd]) for g in range(A_GROUPS)], axis=1)
        s_chunks.append(s_c + bs_ref[...])
    o_a = (u * jnp.concatenate(s_chunks, axis=0)).astype(BF16)
    y_a = _dot(o_a, pa_ref[...])

    seg = tm // SUBLANES
    pitch = seg + SUBLANES
    n_slab = B_WIDTH // LANES
    sub = lax.broadcasted_iota(jnp.int32, (SUBLANES, B_WIDTH), 0)

    xbr = proj(OFF_XB, B_WIDTH)
    for s in range(SUBLANES):
        for l in range(n_slab):
            seg_scr[l, pitch * s:pitch * s + seg, :] = xbr[seg * s:seg * (s + 1), LANES * l:LANES * (l + 1)]
    xs = [jnp.concatenate([seg_scr[l, pl.ds(j, SUBLANES, stride=pitch), :] for l in range(n_slab)], axis=1)
          for j in range(seg)]

    def wrap(cur, prv):
        return pltpu.roll(jnp.where(sub == SUBLANES - 1, prv, cur), 1, 0)

    n_tail = CONV_WIDTH - 1
    ext = [wrap(xs[seg - n_tail + k], tail_scr[SUBLANES * k:SUBLANES * (k + 1), :]) for k in range(n_tail)] + xs
    for k in range(n_tail):
        tail_scr[SUBLANES * k:SUBLANES * (k + 1), :] = xs[seg - n_tail + k]
    conv_w = [jnp.broadcast_to(convw_ref[k:k + 1, :], (SUBLANES, B_WIDTH)) for k in range(CONV_WIDTH)]
    conv_b = jnp.broadcast_to(convb_ref[...], (SUBLANES, B_WIDTH))
    conv_steps = []
    for j in range(seg):
        acc = conv_b
        for k in range(CONV_WIDTH):
            acc = acc + conv_w[k] * ext[j + k]
        conv_steps.append(acc)
    conv = jnp.concatenate(conv_steps, axis=0)
    cb = conv.astype(BF16)
    ri = jnp.concatenate(
        [_dot(cb[:, h * B_HEAD_DIM:(h + 1) * B_HEAD_DIM], wax_ref[h]) for h in range(B_HEADS)],
        axis=1)
    r_parts = [ri[:, (2 * h) * B_HEAD_DIM:(2 * h + 1) * B_HEAD_DIM] for h in range(B_HEADS)]
    i_parts = [ri[:, (2 * h + 1) * B_HEAD_DIM:(2 * h + 2) * B_HEAD_DIM] for h in range(B_HEADS)]
    r = _sigmoid(jnp.concatenate(r_parts, axis=1) + bax_ref[0:1, :])
    i = _sigmoid(jnp.concatenate(i_parts, axis=1) + bax_ref[1:2, :])
    z = -lam_ref[...]
    softplus = jnp.maximum(z, 0.0) + jnp.log(1.0 + jnp.exp(-jnp.abs(z)))
    log_a = (-LRU_C * softplus) * r
    a = jnp.exp(log_a)
    bx = jnp.sqrt(1.0 - a * a) * (i * conv)
    a_steps = [a[SUBLANES * j:SUBLANES * (j + 1), :] for j in range(seg)]
    b_steps = [bx[SUBLANES * j:SUBLANES * (j + 1), :] for j in range(seg)]

    h_loc, p_tot = b_steps[0], a_steps[0]
    for j in range(1, seg):
        h_loc = a_steps[j] * h_loc + b_steps[j]
        p_tot = p_tot * a_steps[j]
    for d in (1, 2, 4):
        m = sub >= d
        h_sh = jnp.where(m, pltpu.roll(h_loc, d, 0), 0.0)
        p_sh = jnp.where(m, pltpu.roll(p_tot, d, 0), 1.0)
        h_loc = p_tot * h_sh + h_loc
        p_tot = p_tot * p_sh
    h_last = hlast_scr[...]
    h_end = h_loc + p_tot * h_last
    hlast_scr[...] = h_end[SUBLANES - 1:SUBLANES, :]
    h = jnp.where(sub == 0, h_last, pltpu.roll(h_end, 1, 0))
    for j in range(seg):
        h = a_steps[j] * h + b_steps[j]
        for l in range(n_slab):
            seg_scr[l, pl.ds(j, SUBLANES, stride=pitch), :] = h[:, LANES * l:LANES * (l + 1)]
    h_tok = jnp.concatenate(
        [jnp.concatenate([seg_scr[l, pitch * s:pitch * s + seg, :] for l in range(n_slab)], axis=1)
         for s in range(SUBLANES)], axis=0)
    o_b = (h_tok * _gelu(proj(OFF_GB, B_WIDTH))).astype(BF16)
    y_b = _dot(o_b, pb_ref[...])

    q = proj(OFF_Q, C_WIDTH).astype(BF16)
    o_heads = []
    for h in range(C_HEADS):
        sl = slice(h * C_HEAD_DIM, (h + 1) * C_HEAD_DIM)
        s = _dot(q[:, sl], kt_ref[sl, :]) * (C_HEAD_DIM ** -0.5)
        p = jnp.exp(s - jnp.max(s, axis=-1, keepdims=True))
        denom = jnp.sum(p, axis=-1, keepdims=True)
        o_heads.append(_dot(p.astype(BF16), vm_ref[:, sl]) / denom)
    o_c = jnp.concatenate(o_heads, axis=1).astype(BF16)
    y_c = _dot(o_c, pc_ref[...])

    y = (_sigmoid(proj(OFF_M, D_MODEL)) * y_a
         + _sigmoid(proj(OFF_M + D_MODEL, D_MODEL)) * y_b
         + _sigmoid(proj(OFF_M + 2 * D_MODEL, D_MODEL)) * y_c)
    m = _dot(y.astype(BF16), wo_ref[...]) + bo_ref[...]
    out_ref[...] = _layer_norm(ALPHA * x + m, ln1_g_ref[...], ln1_b_ref[...])


def _resident(block_shape, index_map):
    return pl.BlockSpec(block_shape, index_map, pipeline_mode=pl.Buffered(1))


def _mixer_call(layer, pre_ln, x, lni_g, lni_b, kt, vm, w):
    bsz, seq, _ = x.shape
    tm = TOKEN_TILE

    def lw(*shape):
        return _resident((None,) + shape, lambda b, s: (layer,) + (0,) * len(shape))

    in_specs = [
        pl.BlockSpec((None, tm, D_MODEL), lambda b, s: (b, s, 0)),
        _resident((1, D_MODEL), lambda b, s: (0, 0)),
        _resident((1, D_MODEL), lambda b, s: (0, 0)),
        pl.BlockSpec((None, None, C_WIDTH, N_MEM), lambda b, s: (layer, b, 0, 0)),
        pl.BlockSpec((None, None, N_MEM, C_WIDTH), lambda b, s: (layer, b, 0, 0)),
        lw(D_MODEL, IN_WIDTH), lw(1, IN_WIDTH),
        lw(1, A_WIDTH), lw(1, A_WIDTH),
        lw(A_GROUPS, CHUNK, CHUNK), lw(CHUNK, A_WIDTH),
        lw(CONV_WIDTH, B_WIDTH), lw(1, B_WIDTH),
        lw(B_HEADS, B_HEAD_DIM, 2 * B_HEAD_DIM), lw(2, B_WIDTH), lw(1, B_WIDTH),
        lw(A_WIDTH, D_MODEL), lw(B_WIDTH, D_MODEL), lw(C_WIDTH, D_MODEL),
        lw(D_MODEL, D_MODEL), lw(1, D_MODEL), lw(1, D_MODEL), lw(1, D_MODEL),
    ]
    return pl.pallas_call(
        functools.partial(_mixer_kernel, pre_ln),
        grid=(bsz, seq // tm),
        in_specs=in_specs,
        out_specs=pl.BlockSpec((None, tm, D_MODEL), lambda b, s: (b, s, 0)),
        out_shape=jax.ShapeDtypeStruct(x.shape, F32),
        scratch_shapes=[
            pltpu.VMEM((B_WIDTH // LANES, tm + SUBLANES * SUBLANES, LANES), F32),
            pltpu.VMEM(((CONV_WIDTH - 1) * SUBLANES, B_WIDTH), F32),
            pltpu.VMEM((1, B_WIDTH), F32),
        ],
        compiler_params=pltpu.CompilerParams(
            dimension_semantics=("arbitrary", "arbitrary"), vmem_limit_bytes=VMEM_LIMIT),
    )(x, lni_g, lni_b, kt, vm, w["w_in"], w["b_in"], w["ln_v_g"], w["ln_v_b"], w["w_s"], w["b_s"],
      w["conv_w"], w["conv_b"], w["w_ax"], w["b_ax"], w["lam"], w["p_a"], w["p_b"], w["p_c"],
      w["w_o"], w["b_o"], w["ln1_g"], w["ln1_b"])


def _first_max_of4(c):
    m = jnp.maximum(jnp.maximum(c[0], c[1]), jnp.maximum(c[2], c[3]))
    idx = jnp.where(c[0] == m, 0, jnp.where(c[1] == m, 1, jnp.where(c[2] == m, 2, 3)))
    return m, idx


def _moe_kernel(x_ref, wr_ref, br_ref, wup_ref, wdn_ref, ln2_g_ref, ln2_b_ref, out_ref,
                xh_scr, g2_scr, pt_scr, f_scr):
    tm = x_ref.shape[0]
    x = x_ref[...]
    x_hi = x.astype(BF16)
    x_lo = (x - x_hi.astype(F32)).astype(BF16)
    xh_scr[...] = x_hi
    lg = _dot(x_hi, wr_ref[...]) + _dot(x_lo, wr_ref[...])
    logits = lg[:, :ROUTER_LANES] + lg[:, ROUTER_LANES:] + br_ref[...]
    lane = lax.broadcasted_iota(jnp.int32, logits.shape, 1)
    neg = -jnp.inf

    gl = jnp.where(lane < N_GROUPS, logits, neg)
    gmax = jnp.max(gl, axis=-1, keepdims=True)
    p_top = 1.0 / jnp.sum(jnp.exp(gl - gmax), axis=-1, keepdims=True)
    g_top = jnp.min(jnp.where(gl == gmax, lane, ROUTER_LANES), axis=-1, keepdims=True)
    in_group = lane == g_top

    e_lo = N_GROUPS + EXPERTS_PER_GROUP * g_top
    cand = [jnp.sum(jnp.where(lane == e_lo + j, logits, 0.0), axis=-1, keepdims=True)
            for j in range(EXPERTS_PER_GROUP)]
    m1, i1 = _first_max_of4(cand)
    m2, i2 = _first_max_of4([jnp.where(i1 == j, neg, cand[j]) for j in range(EXPERTS_PER_GROUP)])
    e2 = jnp.exp(m2 - m1)
    w1 = p_top / (1.0 + e2)
    w2 = w1 * e2
    gate4 = jnp.where(lane == i1, w1, 0.0) + jnp.where(lane == i2, w2, 0.0)

    onehot = jnp.where(in_group, 1.0, 0.0)
    tok_r = lax.broadcasted_iota(jnp.int32, (tm, tm), 0)
    tok_c = lax.broadcasted_iota(jnp.int32, (tm, tm), 1)
    earlier = jnp.where(tok_c < tok_r, 1.0, 0.0).astype(BF16)
    rank = _dot(earlier, onehot.astype(BF16))
    counts = jnp.sum(onehot, axis=0, keepdims=True).astype(jnp.int32)
    ends = []
    end = 0
    base_vec = jnp.zeros((1, ROUTER_LANES), F32)
    lane1 = lax.broadcasted_iota(jnp.int32, (1, ROUTER_LANES), 1)
    for g in range(N_GROUPS):
        if g:
            base_vec = jnp.where(lane1 == g, end.astype(F32), base_vec)
        n_blocks = lax.shift_right_logical(counts[0, g] + (ROW_BLOCK - 1), ROW_BLOCK.bit_length() - 1)
        end = end + n_blocks * ROW_BLOCK
        ends.append(end)
    dest = jnp.sum(jnp.where(in_group, rank + base_vec, 0.0), axis=-1, keepdims=True).astype(jnp.int32)

    dest_row = jnp.broadcast_to(dest, (tm, LANES)).T[0:1, :]
    pt_scr[...] = jnp.where(lax.broadcasted_iota(jnp.int32, (tm, SORTED_ROWS), 1) == dest,
                            1.0, 0.0).astype(BF16)
    g_hi = gate4.astype(BF16)
    g2_scr[...] = jnp.concatenate([g_hi, (gate4 - g_hi.astype(F32)).astype(BF16)], axis=1)
    block_row = lax.broadcasted_iota(jnp.int32, (ROW_BLOCK, tm), 0)

    for b in range(MAX_BLOCKS):
        start = b * ROW_BLOCK
        rows = slice(start, start + ROW_BLOCK)
        grp = sum((start >= ends[g]).astype(jnp.int32) for g in range(N_GROUPS - 1))
        used = start < ends[N_GROUPS - 1]

        @pl.when(used)
        def _():
            perm = jnp.where(block_row + start == dest_row, 1.0, 0.0).astype(BF16)
            xsb = _dot(perm, xh_scr[...]).astype(BF16)
            gs2 = _dot(perm, g2_scr[...])
            gsb = gs2[:, :ROUTER_LANES] + gs2[:, ROUTER_LANES:]
            acts = []
            for j in range(EXPERTS_PER_GROUP):
                h = _dot(xsb, wup_ref[grp * EXPERTS_PER_GROUP + j])
                hg = h[:, :D_EXPERT]
                acts.append((hg * _sigmoid(hg) * h[:, D_EXPERT:] * gsb[:, j:j + 1]).astype(BF16))
            f_scr[rows, :] = _dot(jnp.concatenate(acts, axis=1), wdn_ref[grp]).astype(BF16)

        @pl.when(jnp.logical_not(used))
        def _():
            f_scr[rows, :] = jnp.zeros((ROW_BLOCK, D_MODEL), BF16)

    out_ref[...] = _dot(pt_scr[:, :tm], f_scr[:tm, :])
    for lo in range(tm, SORTED_ROWS, UNSORT_CHUNK):
        @pl.when(lo < ends[N_GROUPS - 1])
        def _():
            out_ref[...] += _dot(pt_scr[:, lo:lo + UNSORT_CHUNK], f_scr[lo:lo + UNSORT_CHUNK, :])
    out_ref[...] = _layer_norm(ALPHA * x_ref[...] + out_ref[...], ln2_g_ref[...], ln2_b_ref[...])


def _moe_call(layer, x2d, w):
    n_tok = x2d.shape[0]
    tm = MOE_TILE

    def lw(*shape):
        return _resident((None,) + shape, lambda t: (layer,) + (0,) * len(shape))

    return pl.pallas_call(
        _moe_kernel,
        grid=(n_tok // tm,),
        in_specs=[
            pl.BlockSpec((tm, D_MODEL), lambda t: (t, 0)),
            lw(D_MODEL, 2 * ROUTER_LANES), lw(1, ROUTER_LANES),
            lw(N_EXPERTS, D_MODEL, 2 * D_EXPERT), lw(N_GROUPS, EXPERTS_PER_GROUP * D_EXPERT, D_MODEL),
            lw(1, D_MODEL), lw(1, D_MODEL),
        ],
        out_specs=pl.BlockSpec((tm, D_MODEL), lambda t: (t, 0)),
        out_shape=jax.ShapeDtypeStruct(x2d.shape, F32),
        scratch_shapes=[
            pltpu.VMEM((tm, D_MODEL), BF16),
            pltpu.VMEM((tm, 2 * ROUTER_LANES), BF16),
            pltpu.VMEM((tm, SORTED_ROWS), BF16),
            pltpu.VMEM((SORTED_ROWS, D_MODEL), BF16),
        ],
        compiler_params=pltpu.CompilerParams(
            dimension_semantics=("arbitrary",), vmem_limit_bytes=VMEM_LIMIT),
    )(x2d, w["wr"], w["br"], w["w_up"], w["w_down"], w["ln2_g"], w["ln2_b"])


def kernel(x, mem, ln_in_g, ln_in_b, ln_mem_g, ln_mem_b, w_in, b_in, ln_v_g, ln_v_b, w_s, b_s, conv_w, conv_b, w_a, b_a, w_x, b_x, lam, w_kv, p_a, p_b, p_c, w_o, b_o, ln1_g, ln1_b, w_rg, b_rg, w_re, b_re, w_up, w_down, ln2_g, ln2_b):
    bsz, seq, _ = x.shape
    assert seq % TOKEN_TILE == 0 and TOKEN_TILE % CHUNK == 0 and (bsz * seq) % MOE_TILE == 0

    def row(p):
        return p[:, None, :]

    wr = jnp.concatenate(
        [w_rg, w_re, jnp.zeros((DEPTH, D_MODEL, ROUTER_LANES - N_GROUPS - N_EXPERTS), F32)], axis=-1)
    wr_hi = wr.astype(BF16)
    w = {
        "w_in": w_in.astype(BF16), "b_in": row(b_in),
        "ln_v_g": row(ln_v_g), "ln_v_b": row(ln_v_b),
        "w_s": w_s.astype(BF16),
        "b_s": jnp.repeat(jnp.swapaxes(b_s, 1, 2), A_WIDTH // A_GROUPS, axis=-1),
        "conv_w": conv_w, "conv_b": row(conv_b),
        "w_ax": jnp.concatenate([w_a, w_x], axis=-1).astype(BF16),
        "b_ax": jnp.stack([b_a, b_x], axis=1), "lam": row(lam),
        "p_a": p_a.astype(BF16), "p_b": p_b.astype(BF16), "p_c": p_c.astype(BF16),
        "w_o": w_o.astype(BF16), "b_o": row(b_o), "ln1_g": row(ln1_g), "ln1_b": row(ln1_b),
        "wr": jnp.concatenate([wr_hi, (wr - wr_hi.astype(F32)).astype(BF16)], axis=-1),
        "br": row(jnp.concatenate(
            [b_rg, b_re, jnp.zeros((DEPTH, ROUTER_LANES - N_GROUPS - N_EXPERTS), F32)], axis=-1)),
        "w_up": w_up.astype(BF16),
        "w_down": w_down.astype(BF16).reshape(DEPTH, N_GROUPS, EXPERTS_PER_GROUP * D_EXPERT, D_MODEL),
        "ln2_g": row(ln2_g), "ln2_b": row(ln2_b),
    }
    lni_g = ln_in_g.reshape(1, D_MODEL)
    lni_b = ln_in_b.reshape(1, D_MODEL)

    kt, vm = _prep_call(mem, ln_mem_g, ln_mem_b, w_kv.astype(BF16))
    for layer in range(DEPTH):
        x = _mixer_call(layer, layer == 0, x, lni_g, lni_b, kt, vm, w)
        x = _moe_call(layer, x.reshape(bsz * seq, D_MODEL), w).reshape(bsz, seq, D_MODEL)
    return x
```

```python
import functools
import math

import jax
import jax.numpy as jnp
from jax import lax
from jax.experimental import pallas as pl
from jax.experimental.pallas import tpu as pltpu

D_MODEL = 1024
DEPTH = 2
N_MEM = 256
CHUNK = 128
A_GROUPS = 8
A_WIDTH = 1024
B_HEADS = 10
B_WIDTH = 1280
B_HEAD_DIM = B_WIDTH // B_HEADS
CONV_WIDTH = 4
LRU_C = 8.0
C_HEADS = 4
C_HEAD_DIM = 256
C_WIDTH = C_HEADS * C_HEAD_DIM
IN_WIDTH = 2 * A_WIDTH + 2 * B_WIDTH + C_WIDTH + 3 * D_MODEL
OFF_U = 0
OFF_V = A_WIDTH
OFF_XB = 2 * A_WIDTH
OFF_GB = OFF_XB + B_WIDTH
OFF_Q = OFF_GB + B_WIDTH
OFF_M = OFF_Q + C_WIDTH
N_GROUPS = 4
EXPERTS_PER_GROUP = 4
N_EXPERTS = 16
D_EXPERT = 256
ALPHA = (2 * DEPTH) ** 0.25
LN_EPS = 1e-5
LOG2_E = 1.0 / math.log(2.0)
GELU_K1 = -2.0 * math.sqrt(2.0 / math.pi) * LOG2_E
GELU_K3 = 0.044715 * GELU_K1

SUBLANES = 8
LANES = 128
ROUTER_LANES = LANES
TOKEN_TILE = 512
MOE_TILE = 512
ROW_BLOCK = 128
MAX_BLOCKS = MOE_TILE // ROW_BLOCK + N_GROUPS
SORTED_ROWS = MAX_BLOCKS * ROW_BLOCK
UNSORT_CHUNK = 2 * ROW_BLOCK
VMEM_LIMIT = 56 * 1024 * 1024

BF16 = jnp.bfloat16
F32 = jnp.float32


def _dot(a, b):
    return jnp.dot(a, b, preferred_element_type=F32)


def _layer_norm(x, g, b):
    mu = jnp.mean(x, axis=-1, keepdims=True)
    xc = x - mu
    var = jnp.mean(xc * xc, axis=-1, keepdims=True)
    return xc * lax.rsqrt(var + LN_EPS) * g + b


def _gelu(x):
    return x * (1.0 / (1.0 + jnp.exp2(x * (GELU_K1 + GELU_K3 * (x * x)))))


def _sigmoid_of_twice(h):
    return 0.5 + 0.5 * jnp.tanh(h)


def _silu(x):
    return x * (1.0 / (1.0 + jnp.exp2(-LOG2_E * x)))


def _sqrt_nonneg(y):
    return jnp.exp2(jnp.log(y) * (0.5 * LOG2_E))


def _prep_kernel(mem_ref, g_ref, b_ref, wkv_ref, kt_ref, v_ref):
    mem_n = _layer_norm(mem_ref[...], g_ref[...], b_ref[...]).astype(BF16)
    kv = _dot(mem_n, wkv_ref[...])
    kt_ref[...] = kv[:, :C_WIDTH].T.astype(BF16)
    v_ref[...] = kv[:, C_WIDTH:].astype(BF16)


def _prep_call(mem, ln_g, ln_b, w_kv_bf):
    bsz = mem.shape[0]
    return pl.pallas_call(
        _prep_kernel,
        grid=(DEPTH, bsz),
        in_specs=[
            pl.BlockSpec((None, N_MEM, D_MODEL), lambda l, b: (b, 0, 0)),
            pl.BlockSpec((1, D_MODEL), lambda l, b: (0, 0)),
            pl.BlockSpec((1, D_MODEL), lambda l, b: (0, 0)),
            pl.BlockSpec((None, D_MODEL, 2 * C_WIDTH), lambda l, b: (l, 0, 0)),
        ],
        out_specs=[
            pl.BlockSpec((None, None, C_WIDTH, N_MEM), lambda l, b: (l, b, 0, 0)),
            pl.BlockSpec((None, None, N_MEM, C_WIDTH), lambda l, b: (l, b, 0, 0)),
        ],
        out_shape=[
            jax.ShapeDtypeStruct((DEPTH, bsz, C_WIDTH, N_MEM), BF16),
            jax.ShapeDtypeStruct((DEPTH, bsz, N_MEM, C_WIDTH), BF16),
        ],
        compiler_params=pltpu.CompilerParams(
            dimension_semantics=("arbitrary", "arbitrary"), vmem_limit_bytes=VMEM_LIMIT),
    )(mem, ln_g.reshape(1, D_MODEL), ln_b.reshape(1, D_MODEL), w_kv_bf)


def _mixer_kernel(pre_ln, x_ref, lni_g_ref, lni_b_ref, kt_ref, vm_ref, w_in_ref, b_in_ref,
                  lnv_g_ref, lnv_b_ref, ws_ref, bs_ref, convw_ref, convb_ref, wax_ref, bax_ref,
                  lam_ref, pa_ref, pb_ref, pc_ref, wo_ref, bo_ref, ln1_g_ref, ln1_b_ref,
                  out_ref, seg_scr, tail_scr, hlast_scr):
    tm = x_ref.shape[0]

    @pl.when(pl.program_id(1) == 0)
    def _():
        tail_scr[...] = jnp.zeros(tail_scr.shape, F32)
        hlast_scr[...] = jnp.zeros((1, B_WIDTH), F32)

    x = x_ref[...]
    if pre_ln:
        x = _layer_norm(x, lni_g_ref[...], lni_b_ref[...])
    xb = x.astype(BF16)

    def proj(lo, width):
        return _dot(xb, w_in_ref[:, lo:lo + width]) + b_in_ref[:, lo:lo + width]

    v_pre = proj(OFF_V, A_WIDTH)
    xbr = proj(OFF_XB, B_WIDTH)
    vn = _layer_norm(_gelu(v_pre), lnv_g_ref[...], lnv_b_ref[...]).astype(BF16)
    u_pre = proj(OFF_U, A_WIDTH)

    seg = tm // SUBLANES
    pitch = seg + SUBLANES
    n_slab = B_WIDTH // LANES
    sub = lax.broadcasted_iota(jnp.int32, (SUBLANES, B_WIDTH), 0)
    for s in range(SUBLANES):
        for l in range(n_slab):
            seg_scr[l, pitch * s:pitch * s + seg, :] = xbr[seg * s:seg * (s + 1), LANES * l:LANES * (l + 1)]
    xs = [jnp.concatenate([seg_scr[l, pl.ds(j, SUBLANES, stride=pitch), :] for l in range(n_slab)], axis=1)
          for j in range(seg)]

    def wrap(cur, prv):
        return pltpu.roll(jnp.where(sub == SUBLANES - 1, prv, cur), 1, 0)

    n_tail = CONV_WIDTH - 1
    ext = [wrap(xs[seg - n_tail + k], tail_scr[SUBLANES * k:SUBLANES * (k + 1), :]) for k in range(n_tail)] + xs
    for k in range(n_tail):
        tail_scr[SUBLANES * k:SUBLANES * (k + 1), :] = xs[seg - n_tail + k]
    conv_w = [jnp.broadcast_to(convw_ref[k:k + 1, :], (SUBLANES, B_WIDTH)) for k in range(CONV_WIDTH)]
    conv_b = jnp.broadcast_to(convb_ref[...], (SUBLANES, B_WIDTH))
    conv_steps = []
    for j in range(seg):
        acc = conv_b
        for k in range(CONV_WIDTH):
            acc = acc + conv_w[k] * ext[j + k]
        conv_steps.append(acc)
    conv = jnp.concatenate(conv_steps, axis=0)
    cb = conv.astype(BF16)
    gb_pre = proj(OFF_GB, B_WIDTH)
    u = _gelu(u_pre)

    row = lax.broadcasted_iota(jnp.int32, (CHUNK, CHUNK), 0)
    col = lax.broadcasted_iota(jnp.int32, (CHUNK, CHUNK), 1)
    causal = col <= row
    ws = [jnp.where(causal, ws_ref[g], jnp.zeros((CHUNK, CHUNK), BF16)) for g in range(A_GROUPS)]
    s_chunks = []
    for c in range(tm // CHUNK):
        vc = vn[c * CHUNK:(c + 1) * CHUNK, :]
        gd = A_WIDTH // A_GROUPS
        s_c = jnp.concatenate(
            [_dot(ws[g], vc[:, g * gd:(g + 1) * gd]) for g in range(A_GROUPS)], axis=1)
        s_chunks.append(s_c + bs_ref[...])
    gate_b = _gelu(gb_pre)
    q = proj(OFF_Q, C_WIDTH).astype(BF16)
    o_a = (u * jnp.concatenate(s_chunks, axis=0)).astype(BF16)
    ri = jnp.concatenate(
        [_dot(cb[:, h * B_HEAD_DIM:(h + 1) * B_HEAD_DIM], wax_ref[h]) for h in range(B_HEADS)],
        axis=1)
    m_a = proj(OFF_M, D_MODEL)

    r_parts = [ri[:, (2 * h) * B_HEAD_DIM:(2 * h + 1) * B_HEAD_DIM] for h in range(B_HEADS)]
    i_parts = [ri[:, (2 * h + 1) * B_HEAD_DIM:(2 * h + 2) * B_HEAD_DIM] for h in range(B_HEADS)]
    r = _sigmoid_of_twice(jnp.concatenate(r_parts, axis=1) + bax_ref[0:1, :])
    i = _sigmoid_of_twice(jnp.concatenate(i_parts, axis=1) + bax_ref[1:2, :])
    z = -lam_ref[...]
    softplus = jnp.maximum(z, 0.0) + jnp.log(1.0 + jnp.exp(-jnp.abs(z)))
    log_a = (-LRU_C * softplus) * r
    a = jnp.exp(log_a)
    bx = _sqrt_nonneg(1.0 - a * a) * (i * conv)
    a_steps = [a[SUBLANES * j:SUBLANES * (j + 1), :] for j in range(seg)]
    b_steps = [bx[SUBLANES * j:SUBLANES * (j + 1), :] for j in range(seg)]
    y_a = _dot(o_a, pa_ref[...])
    scores = [_dot(q[:, h * C_HEAD_DIM:(h + 1) * C_HEAD_DIM], kt_ref[h * C_HEAD_DIM:(h + 1) * C_HEAD_DIM, :])
              for h in range(C_HEADS)]

    h_loc, p_tot = b_steps[0], a_steps[0]
    for j in range(1, seg):
        h_loc = a_steps[j] * h_loc + b_steps[j]
        p_tot = p_tot * a_steps[j]
    for d in (1, 2, 4):
        m = sub >= d
        h_sh = jnp.where(m, pltpu.roll(h_loc, d, 0), 0.0)
        p_sh = jnp.where(m, pltpu.roll(p_tot, d, 0), 1.0)
        h_loc = p_tot * h_sh + h_loc
        p_tot = p_tot * p_sh
    h_last = hlast_scr[...]
    h_end = h_loc + p_tot * h_last
    hlast_scr[...] = h_end[SUBLANES - 1:SUBLANES, :]
    h = jnp.where(sub == 0, h_last, pltpu.roll(h_end, 1, 0))
    for j in range(seg):
        h = a_steps[j] * h + b_steps[j]
        for l in range(n_slab):
            seg_scr[l, pl.ds(j, SUBLANES, stride=pitch), :] = h[:, LANES * l:LANES * (l + 1)]
    h_tok = jnp.concatenate(
        [jnp.concatenate([seg_scr[l, pitch * s:pitch * s + seg, :] for l in range(n_slab)], axis=1)
         for s in range(SUBLANES)], axis=0)
    o_b = (h_tok * gate_b).astype(BF16)

    m_b = proj(OFF_M + D_MODEL, D_MODEL)
    probs, denoms = [], []
    for h in range(C_HEADS):
        s = scores[h] * (C_HEAD_DIM ** -0.5)
        p = jnp.exp(s - jnp.max(s, axis=-1, keepdims=True))
        denoms.append(jnp.sum(p, axis=-1, keepdims=True))
        probs.append(p.astype(BF16))
    m_c = proj(OFF_M + 2 * D_MODEL, D_MODEL)
    y_b = _dot(o_b, pb_ref[...])
    o_c = jnp.concatenate(
        [_dot(probs[h], vm_ref[:, h * C_HEAD_DIM:(h + 1) * C_HEAD_DIM]) / denoms[h] for h in range(C_HEADS)],
        axis=1).astype(BF16)
    y_ab = _sigmoid_of_twice(m_a) * y_a + _sigmoid_of_twice(m_b) * y_b
    y_c = _dot(o_c, pc_ref[...])

    y = y_ab + _sigmoid_of_twice(m_c) * y_c
    m = _dot(y.astype(BF16), wo_ref[...]) + bo_ref[...]
    out_ref[...] = _layer_norm(ALPHA * x + m, ln1_g_ref[...], ln1_b_ref[...])


def _resident(block_shape, index_map):
    return pl.BlockSpec(block_shape, index_map, pipeline_mode=pl.Buffered(1))


def _mixer_call(layer, pre_ln, x, lni_g, lni_b, kt, vm, w):
    bsz, seq, _ = x.shape
    tm = TOKEN_TILE

    def lw(*shape):
        return _resident((None,) + shape, lambda b, s: (layer,) + (0,) * len(shape))

    in_specs = [
        pl.BlockSpec((None, tm, D_MODEL), lambda b, s: (b, s, 0)),
        _resident((1, D_MODEL), lambda b, s: (0, 0)),
        _resident((1, D_MODEL), lambda b, s: (0, 0)),
        pl.BlockSpec((None, None, C_WIDTH, N_MEM), lambda b, s: (layer, b, 0, 0)),
        pl.BlockSpec((None, None, N_MEM, C_WIDTH), lambda b, s: (layer, b, 0, 0)),
        lw(D_MODEL, IN_WIDTH), lw(1, IN_WIDTH),
        lw(1, A_WIDTH), lw(1, A_WIDTH),
        lw(A_GROUPS, CHUNK, CHUNK), lw(CHUNK, A_WIDTH),
        lw(CONV_WIDTH, B_WIDTH), lw(1, B_WIDTH),
        lw(B_HEADS, B_HEAD_DIM, 2 * B_HEAD_DIM), lw(2, B_WIDTH), lw(1, B_WIDTH),
        lw(A_WIDTH, D_MODEL), lw(B_WIDTH, D_MODEL), lw(C_WIDTH, D_MODEL),
        lw(D_MODEL, D_MODEL), lw(1, D_MODEL), lw(1, D_MODEL), lw(1, D_MODEL),
    ]
    return pl.pallas_call(
        functools.partial(_mixer_kernel, pre_ln),
        grid=(bsz, seq // tm),
        in_specs=in_specs,
        out_specs=pl.BlockSpec((None, tm, D_MODEL), lambda b, s: (b, s, 0)),
        out_shape=jax.ShapeDtypeStruct(x.shape, F32),
        scratch_shapes=[
            pltpu.VMEM((B_WIDTH // LANES, tm + SUBLANES * SUBLANES, LANES), F32),
            pltpu.VMEM(((CONV_WIDTH - 1) * SUBLANES, B_WIDTH), F32),
            pltpu.VMEM((1, B_WIDTH), F32),
        ],
        compiler_params=pltpu.CompilerParams(
            dimension_semantics=("arbitrary", "arbitrary"), vmem_limit_bytes=VMEM_LIMIT),
    )(x, lni_g, lni_b, kt, vm, w["w_in"], w["b_in"], w["ln_v_g"], w["ln_v_b"], w["w_s"], w["b_s"],
      w["conv_w"], w["conv_b"], w["w_ax"], w["b_ax"], w["lam"], w["p_a"], w["p_b"], w["p_c"],
      w["w_o"], w["b_o"], w["ln1_g"], w["ln1_b"])


def _first_max_of4(c):
    m = jnp.maximum(jnp.maximum(c[0], c[1]), jnp.maximum(c[2], c[3]))
    idx = jnp.where(c[0] == m, 0, jnp.where(c[1] == m, 1, jnp.where(c[2] == m, 2, 3)))
    return m, idx


def _moe_kernel(x_ref, wr_ref, br_ref, wup_ref, wdn_ref, ln2_g_ref, ln2_b_ref, out_ref,
                xh_scr, g2_scr, pt_scr, f_scr):
    tm = x_ref.shape[0]
    x = x_ref[...]
    x_hi = x.astype(BF16)
    x_lo = (x - x_hi.astype(F32)).astype(BF16)
    xh_scr[...] = x_hi
    lg = _dot(x_hi, wr_ref[...]) + _dot(x_lo, wr_ref[...])
    logits = lg[:, :ROUTER_LANES] + lg[:, ROUTER_LANES:] + br_ref[...]
    lane = lax.broadcasted_iota(jnp.int32, logits.shape, 1)
    neg = -jnp.inf

    gl = jnp.where(lane < N_GROUPS, logits, neg)
    gmax = jnp.max(gl, axis=-1, keepdims=True)
    p_top = 1.0 / jnp.sum(jnp.exp(gl - gmax), axis=-1, keepdims=True)
    g_top = jnp.min(jnp.where(gl == gmax, lane, ROUTER_LANES), axis=-1, keepdims=True)
    in_group = lane == g_top

    e_lo = N_GROUPS + EXPERTS_PER_GROUP * g_top
    cand = [jnp.sum(jnp.where(lane == e_lo + j, logits, 0.0), axis=-1, keepdims=True)
            for j in range(EXPERTS_PER_GROUP)]
    m1, i1 = _first_max_of4(cand)
    m2, i2 = _first_max_of4([jnp.where(i1 == j, neg, cand[j]) for j in range(EXPERTS_PER_GROUP)])
    e2 = jnp.exp(m2 - m1)
    w1 = p_top / (1.0 + e2)
    w2 = w1 * e2
    gate4 = jnp.where(lane == i1, w1, 0.0) + jnp.where(lane == i2, w2, 0.0)

    onehot = jnp.where(in_group, 1.0, 0.0)
    tok_r = lax.broadcasted_iota(jnp.int32, (tm, tm), 0)
    tok_c = lax.broadcasted_iota(jnp.int32, (tm, tm), 1)
    earlier = jnp.where(tok_c < tok_r, 1.0, 0.0).astype(BF16)
    rank = _dot(earlier, onehot.astype(BF16))
    counts = jnp.sum(onehot, axis=0, keepdims=True).astype(jnp.int32)
    ends = []
    end = 0
    base_vec = jnp.zeros((1, ROUTER_LANES), F32)
    lane1 = lax.broadcasted_iota(jnp.int32, (1, ROUTER_LANES), 1)
    for g in range(N_GROUPS):
        if g:
            base_vec = jnp.where(lane1 == g, end.astype(F32), base_vec)
        n_blocks = lax.shift_right_logical(counts[0, g] + (ROW_BLOCK - 1), ROW_BLOCK.bit_length() - 1)
        end = end + n_blocks * ROW_BLOCK
        ends.append(end)
    dest = jnp.sum(jnp.where(in_group, rank + base_vec, 0.0), axis=-1, keepdims=True).astype(jnp.int32)

    dest_row = jnp.broadcast_to(dest, (tm, LANES)).T[0:1, :]
    pt_scr[...] = jnp.where(lax.broadcasted_iota(jnp.int32, (tm, SORTED_ROWS), 1) == dest,
                            1.0, 0.0).astype(BF16)
    g_hi = gate4.astype(BF16)
    g2_scr[...] = jnp.concatenate([g_hi, (gate4 - g_hi.astype(F32)).astype(BF16)], axis=1)
    block_row = lax.broadcasted_iota(jnp.int32, (ROW_BLOCK, tm), 0)

    for b in range(MAX_BLOCKS):
        start = b * ROW_BLOCK
        rows = slice(start, start + ROW_BLOCK)
        grp = sum((start >= ends[g]).astype(jnp.int32) for g in range(N_GROUPS - 1))
        used = start < ends[N_GROUPS - 1]

        @pl.when(used)
        def _():
            perm = jnp.where(block_row + start == dest_row, 1.0, 0.0).astype(BF16)
            xsb = _dot(perm, xh_scr[...]).astype(BF16)
            gs2 = _dot(perm, g2_scr[...])
            gsb = gs2[:, :ROUTER_LANES] + gs2[:, ROUTER_LANES:]
            acts = []
            for j in range(EXPERTS_PER_GROUP):
                h = _dot(xsb, wup_ref[grp * EXPERTS_PER_GROUP + j])
                hg = h[:, :D_EXPERT]
                acts.append((_silu(hg) * h[:, D_EXPERT:] * gsb[:, j:j + 1]).astype(BF16))
            f_scr[rows, :] = _dot(jnp.concatenate(acts, axis=1), wdn_ref[grp]).astype(BF16)

        @pl.when(jnp.logical_not(used))
        def _():
            f_scr[rows, :] = jnp.zeros((ROW_BLOCK, D_MODEL), BF16)

    out_ref[...] = _dot(pt_scr[:, :tm], f_scr[:tm, :])
    for lo in range(tm, SORTED_ROWS, UNSORT_CHUNK):
        @pl.when(lo < ends[N_GROUPS - 1])
        def _():
            out_ref[...] += _dot(pt_scr[:, lo:lo + UNSORT_CHUNK], f_scr[lo:lo + UNSORT_CHUNK, :])
    out_ref[...] = _layer_norm(ALPHA * x_ref[...] + out_ref[...], ln2_g_ref[...], ln2_b_ref[...])


def _moe_call(layer, x2d, w):
    n_tok = x2d.shape[0]
    tm = MOE_TILE

    def lw(*shape):
        return _resident((None,) + shape, lambda t: (layer,) + (0,) * len(shape))

    return pl.pallas_call(
        _moe_kernel,
        grid=(n_tok // tm,),
        in_specs=[
            pl.BlockSpec((tm, D_MODEL), lambda t: (t, 0)),
            lw(D_MODEL, 2 * ROUTER_LANES), lw(1, ROUTER_LANES),
            lw(N_EXPERTS, D_MODEL, 2 * D_EXPERT), lw(N_GROUPS, EXPERTS_PER_GROUP * D_EXPERT, D_MODEL),
            lw(1, D_MODEL), lw(1, D_MODEL),
        ],
        out_specs=pl.BlockSpec((tm, D_MODEL), lambda t: (t, 0)),
        out_shape=jax.ShapeDtypeStruct(x2d.shape, F32),
        scratch_shapes=[
            pltpu.VMEM((tm, D_MODEL), BF16),
            pltpu.VMEM((tm, 2 * ROUTER_LANES), BF16),
            pltpu.VMEM((tm, SORTED_ROWS), BF16),
            pltpu.VMEM((SORTED_ROWS, D_MODEL), BF16),
        ],
        compiler_params=pltpu.CompilerParams(
            dimension_semantics=("arbitrary",), vmem_limit_bytes=VMEM_LIMIT),
    )(x2d, w["wr"], w["br"], w["w_up"], w["w_down"], w["ln2_g"], w["ln2_b"])


def kernel(x, mem, ln_in_g, ln_in_b, ln_mem_g, ln_mem_b, w_in, b_in, ln_v_g, ln_v_b, w_s, b_s, conv_w, conv_b, w_a, b_a, w_x, b_x, lam, w_kv, p_a, p_b, p_c, w_o, b_o, ln1_g, ln1_b, w_rg, b_rg, w_re, b_re, w_up, w_down, ln2_g, ln2_b):
    bsz, seq, _ = x.shape
    assert seq % TOKEN_TILE == 0 and TOKEN_TILE % CHUNK == 0 and (bsz * seq) % MOE_TILE == 0

    def row(p):
        return p[:, None, :]

    wr = jnp.concatenate(
        [w_rg, w_re, jnp.zeros((DEPTH, D_MODEL, ROUTER_LANES - N_GROUPS - N_EXPERTS), F32)], axis=-1)
    wr_hi = wr.astype(BF16)
    in_scale = jnp.where(jnp.arange(IN_WIDTH) >= OFF_M, 0.5, 1.0).astype(F32)
    w = {
        "w_in": (w_in * in_scale).astype(BF16), "b_in": row(b_in * in_scale),
        "ln_v_g": row(ln_v_g), "ln_v_b": row(ln_v_b),
        "w_s": w_s.astype(BF16),
        "b_s": jnp.repeat(jnp.swapaxes(b_s, 1, 2), A_WIDTH // A_GROUPS, axis=-1),
        "conv_w": conv_w, "conv_b": row(conv_b),
        "w_ax": (0.5 * jnp.concatenate([w_a, w_x], axis=-1)).astype(BF16),
        "b_ax": 0.5 * jnp.stack([b_a, b_x], axis=1), "lam": row(lam),
        "p_a": p_a.astype(BF16), "p_b": p_b.astype(BF16), "p_c": p_c.astype(BF16),
        "w_o": w_o.astype(BF16), "b_o": row(b_o), "ln1_g": row(ln1_g), "ln1_b": row(ln1_b),
        "wr": jnp.concatenate([wr_hi, (wr - wr_hi.astype(F32)).astype(BF16)], axis=-1),
        "br": row(jnp.concatenate(
            [b_rg, b_re, jnp.zeros((DEPTH, ROUTER_LANES - N_GROUPS - N_EXPERTS), F32)], axis=-1)),
        "w_up": w_up.astype(BF16),
        "w_down": w_down.astype(BF16).reshape(DEPTH, N_GROUPS, EXPERTS_PER_GROUP * D_EXPERT, D_MODEL),
        "ln2_g": row(ln2_g), "ln2_b": row(ln2_b),
    }
    lni_g = ln_in_g.reshape(1, D_MODEL)
    lni_b = ln_in_b.reshape(1, D_MODEL)

    kt, vm = _prep_call(mem, ln_mem_g, ln_mem_b, w_kv.astype(BF16))
    for layer in range(DEPTH):
        x = _mixer_call(layer, layer == 0, x, lni_g, lni_b, kt, vm, w)
        x = _moe_call(layer, x.reshape(bsz * seq, D_MODEL), w).reshape(bsz, seq, D_MODEL)
    return x
```

```python
import functools
import math

import jax
import jax.numpy as jnp
from jax import lax
from jax.experimental import pallas as pl
from jax.experimental.pallas import tpu as pltpu

D_MODEL = 1024
DEPTH = 2
N_MEM = 256
CHUNK = 128
A_GROUPS = 8
A_WIDTH = 1024
B_HEADS = 10
B_WIDTH = 1280
B_HEAD_DIM = B_WIDTH // B_HEADS
CONV_WIDTH = 4
LRU_C = 8.0
C_HEADS = 4
C_HEAD_DIM = 256
C_WIDTH = C_HEADS * C_HEAD_DIM
IN_WIDTH = 2 * A_WIDTH + 2 * B_WIDTH + C_WIDTH + 3 * D_MODEL
OFF_U = 0
OFF_V = A_WIDTH
OFF_XB = 2 * A_WIDTH
OFF_GB = OFF_XB + B_WIDTH
OFF_Q = OFF_GB + B_WIDTH
OFF_M = OFF_Q + C_WIDTH
N_GROUPS = 4
EXPERTS_PER_GROUP = 4
N_EXPERTS = 16
D_EXPERT = 256
ALPHA = (2 * DEPTH) ** 0.25
LN_EPS = 1e-5
LOG2_E = 1.0 / math.log(2.0)
GELU_K1 = -2.0 * math.sqrt(2.0 / math.pi) * LOG2_E
GELU_K3 = 0.044715 * GELU_K1

SUBLANES = 8
LANES = 128
ROUTER_LANES = LANES
TOKEN_TILE = 512
MOE_TILE = 512
ROW_BLOCK = 128
MAX_BLOCKS = MOE_TILE // ROW_BLOCK + N_GROUPS
SORTED_ROWS = MAX_BLOCKS * ROW_BLOCK
FINISH_ROWS = 256
VMEM_LIMIT = 56 * 1024 * 1024

BF16 = jnp.bfloat16
F32 = jnp.float32


def _dot(a, b):
    return jnp.dot(a, b, preferred_element_type=F32)


def _layer_norm(x, g, b):
    mu = jnp.mean(x, axis=-1, keepdims=True)
    xc = x - mu
    var = jnp.mean(xc * xc, axis=-1, keepdims=True)
    return xc * lax.rsqrt(var + LN_EPS) * g + b


def _gelu(x):
    return x * (1.0 / (1.0 + jnp.exp2(x * (GELU_K1 + GELU_K3 * (x * x)))))


def _sigmoid_of_twice(h):
    return 0.5 + 0.5 * jnp.tanh(h)


def _silu(x):
    return x * (1.0 / (1.0 + jnp.exp2(-LOG2_E * x)))


def _sqrt_nonneg(y):
    return jnp.exp2(jnp.log(y) * (0.5 * LOG2_E))


def _prep_kernel(mem_ref, g_ref, b_ref, wkv_ref, kt_ref, v_ref):
    mem_n = _layer_norm(mem_ref[...], g_ref[...], b_ref[...]).astype(BF16)
    kv = _dot(mem_n, wkv_ref[...])
    kt_ref[...] = kv[:, :C_WIDTH].T.astype(BF16)
    v_ref[...] = kv[:, C_WIDTH:].astype(BF16)


def _prep_call(mem, ln_g, ln_b, w_kv_bf):
    bsz = mem.shape[0]
    return pl.pallas_call(
        _prep_kernel,
        grid=(DEPTH, bsz),
        in_specs=[
            pl.BlockSpec((None, N_MEM, D_MODEL), lambda l, b: (b, 0, 0)),
            pl.BlockSpec((1, D_MODEL), lambda l, b: (0, 0)),
            pl.BlockSpec((1, D_MODEL), lambda l, b: (0, 0)),
            pl.BlockSpec((None, D_MODEL, 2 * C_WIDTH), lambda l, b: (l, 0, 0)),
        ],
        out_specs=[
            pl.BlockSpec((None, None, C_WIDTH, N_MEM), lambda l, b: (l, b, 0, 0)),
            pl.BlockSpec((None, None, N_MEM, C_WIDTH), lambda l, b: (l, b, 0, 0)),
        ],
        out_shape=[
            jax.ShapeDtypeStruct((DEPTH, bsz, C_WIDTH, N_MEM), BF16),
            jax.ShapeDtypeStruct((DEPTH, bsz, N_MEM, C_WIDTH), BF16),
        ],
        compiler_params=pltpu.CompilerParams(
            dimension_semantics=("arbitrary", "arbitrary"), vmem_limit_bytes=VMEM_LIMIT),
    )(mem, ln_g.reshape(1, D_MODEL), ln_b.reshape(1, D_MODEL), w_kv_bf)


def _mixer_kernel(pre_ln, x_ref, lni_g_ref, lni_b_ref, kt_ref, vm_ref, w_in_ref, b_in_ref,
                  lnv_g_ref, lnv_b_ref, ws_ref, bs_ref, convw_ref, convb_ref, wax_ref, bax_ref,
                  lam_ref, pa_ref, pb_ref, pc_ref, wo_ref, bo_ref, ln1_g_ref, ln1_b_ref,
                  out_ref, seg_scr, tail_scr, hlast_scr):
    tm = x_ref.shape[0]

    @pl.when(pl.program_id(1) == 0)
    def _():
        tail_scr[...] = jnp.zeros(tail_scr.shape, F32)
        hlast_scr[...] = jnp.zeros((1, B_WIDTH), F32)

    x = x_ref[...]
    if pre_ln:
        x = _layer_norm(x, lni_g_ref[...], lni_b_ref[...])
    xb = x.astype(BF16)

    def proj(lo, width):
        return _dot(xb, w_in_ref[:, lo:lo + width]) + b_in_ref[:, lo:lo + width]

    v_pre = proj(OFF_V, A_WIDTH)
    xbr = proj(OFF_XB, B_WIDTH)
    vn = _layer_norm(_gelu(v_pre), lnv_g_ref[...], lnv_b_ref[...]).astype(BF16)
    u_pre = proj(OFF_U, A_WIDTH)

    seg = tm // SUBLANES
    pitch = seg + SUBLANES
    n_slab = B_WIDTH // LANES
    sub = lax.broadcasted_iota(jnp.int32, (SUBLANES, B_WIDTH), 0)
    for s in range(SUBLANES):
        for l in range(n_slab):
            seg_scr[l, pitch * s:pitch * s + seg, :] = xbr[seg * s:seg * (s + 1), LANES * l:LANES * (l + 1)]
    xs = [jnp.concatenate([seg_scr[l, pl.ds(j, SUBLANES, stride=pitch), :] for l in range(n_slab)], axis=1)
          for j in range(seg)]

    def wrap(cur, prv):
        return pltpu.roll(jnp.where(sub == SUBLANES - 1, prv, cur), 1, 0)

    n_tail = CONV_WIDTH - 1
    ext = [wrap(xs[seg - n_tail + k], tail_scr[SUBLANES * k:SUBLANES * (k + 1), :]) for k in range(n_tail)] + xs
    for k in range(n_tail):
        tail_scr[SUBLANES * k:SUBLANES * (k + 1), :] = xs[seg - n_tail + k]
    conv_w = [jnp.broadcast_to(convw_ref[k:k + 1, :], (SUBLANES, B_WIDTH)) for k in range(CONV_WIDTH)]
    conv_b = jnp.broadcast_to(convb_ref[...], (SUBLANES, B_WIDTH))
    conv_steps = []
    for j in range(seg):
        acc = conv_b
        for k in range(CONV_WIDTH):
            acc = acc + conv_w[k] * ext[j + k]
        conv_steps.append(acc)
    conv = jnp.concatenate(conv_steps, axis=0)
    cb = conv.astype(BF16)
    gb_pre = proj(OFF_GB, B_WIDTH)
    u = _gelu(u_pre)

    row = lax.broadcasted_iota(jnp.int32, (CHUNK, CHUNK), 0)
    col = lax.broadcasted_iota(jnp.int32, (CHUNK, CHUNK), 1)
    causal = col <= row
    ws = [jnp.where(causal, ws_ref[g], jnp.zeros((CHUNK, CHUNK), BF16)) for g in range(A_GROUPS)]
    s_chunks = []
    for c in range(tm // CHUNK):
        vc = vn[c * CHUNK:(c + 1) * CHUNK, :]
        gd = A_WIDTH // A_GROUPS
        s_c = jnp.concatenate(
            [_dot(ws[g], vc[:, g * gd:(g + 1) * gd]) for g in range(A_GROUPS)], axis=1)
        s_chunks.append(s_c + bs_ref[...])
    gate_b = _gelu(gb_pre)
    q = proj(OFF_Q, C_WIDTH).astype(BF16)
    o_a = (u * jnp.concatenate(s_chunks, axis=0)).astype(BF16)
    ri = jnp.concatenate(
        [_dot(cb[:, h * B_HEAD_DIM:(h + 1) * B_HEAD_DIM], wax_ref[h]) for h in range(B_HEADS)],
        axis=1)
    m_a = proj(OFF_M, D_MODEL)

    r_parts = [ri[:, (2 * h) * B_HEAD_DIM:(2 * h + 1) * B_HEAD_DIM] for h in range(B_HEADS)]
    i_parts = [ri[:, (2 * h + 1) * B_HEAD_DIM:(2 * h + 2) * B_HEAD_DIM] for h in range(B_HEADS)]
    r = _sigmoid_of_twice(jnp.concatenate(r_parts, axis=1) + bax_ref[0:1, :])
    i = _sigmoid_of_twice(jnp.concatenate(i_parts, axis=1) + bax_ref[1:2, :])
    z = -lam_ref[...]
    softplus = jnp.maximum(z, 0.0) + jnp.log(1.0 + jnp.exp(-jnp.abs(z)))
    log_a = (-LRU_C * softplus) * r
    a = jnp.exp(log_a)
    bx = _sqrt_nonneg(1.0 - a * a) * (i * conv)
    a_steps = [a[SUBLANES * j:SUBLANES * (j + 1), :] for j in range(seg)]
    b_steps = [bx[SUBLANES * j:SUBLANES * (j + 1), :] for j in range(seg)]
    y_a = _dot(o_a, pa_ref[...])
    scores = [_dot(q[:, h * C_HEAD_DIM:(h + 1) * C_HEAD_DIM], kt_ref[h * C_HEAD_DIM:(h + 1) * C_HEAD_DIM, :])
              for h in range(C_HEADS)]

    h_loc, p_tot = b_steps[0], a_steps[0]
    for j in range(1, seg):
        h_loc = a_steps[j] * h_loc + b_steps[j]
        p_tot = p_tot * a_steps[j]
    for d in (1, 2, 4):
        m = sub >= d
        h_sh = jnp.where(m, pltpu.roll(h_loc, d, 0), 0.0)
        p_sh = jnp.where(m, pltpu.roll(p_tot, d, 0), 1.0)
        h_loc = p_tot * h_sh + h_loc
        p_tot = p_tot * p_sh
    h_last = hlast_scr[...]
    h_end = h_loc + p_tot * h_last
    hlast_scr[...] = h_end[SUBLANES - 1:SUBLANES, :]
    h = jnp.where(sub == 0, h_last, pltpu.roll(h_end, 1, 0))
    for j in range(seg):
        h = a_steps[j] * h + b_steps[j]
        for l in range(n_slab):
            seg_scr[l, pl.ds(j, SUBLANES, stride=pitch), :] = h[:, LANES * l:LANES * (l + 1)]
    h_tok = jnp.concatenate(
        [jnp.concatenate([seg_scr[l, pitch * s:pitch * s + seg, :] for l in range(n_slab)], axis=1)
         for s in range(SUBLANES)], axis=0)
    o_b = (h_tok * gate_b).astype(BF16)

    m_b = proj(OFF_M + D_MODEL, D_MODEL)
    probs, denoms = [], []
    for h in range(C_HEADS):
        s = scores[h] * (C_HEAD_DIM ** -0.5)
        p = jnp.exp(s - jnp.max(s, axis=-1, keepdims=True))
        denoms.append(jnp.sum(p, axis=-1, keepdims=True))
        probs.append(p.astype(BF16))
    m_c = proj(OFF_M + 2 * D_MODEL, D_MODEL)
    y_b = _dot(o_b, pb_ref[...])
    o_c = jnp.concatenate(
        [_dot(probs[h], vm_ref[:, h * C_HEAD_DIM:(h + 1) * C_HEAD_DIM]) / denoms[h] for h in range(C_HEADS)],
        axis=1).astype(BF16)
    y_ab = _sigmoid_of_twice(m_a) * y_a + _sigmoid_of_twice(m_b) * y_b
    y_c = _dot(o_c, pc_ref[...])

    y = y_ab + _sigmoid_of_twice(m_c) * y_c
    m = _dot(y.astype(BF16), wo_ref[...]) + bo_ref[...]
    out_ref[...] = _layer_norm(ALPHA * x + m, ln1_g_ref[...], ln1_b_ref[...])


def _resident(block_shape, index_map):
    return pl.BlockSpec(block_shape, index_map, pipeline_mode=pl.Buffered(1))


def _mixer_call(layer, pre_ln, x, lni_g, lni_b, kt, vm, w):
    bsz, seq, _ = x.shape
    tm = TOKEN_TILE

    def lw(*shape):
        return _resident((None,) + shape, lambda b, s: (layer,) + (0,) * len(shape))

    in_specs = [
        pl.BlockSpec((None, tm, D_MODEL), lambda b, s: (b, s, 0)),
        _resident((1, D_MODEL), lambda b, s: (0, 0)),
        _resident((1, D_MODEL), lambda b, s: (0, 0)),
        pl.BlockSpec((None, None, C_WIDTH, N_MEM), lambda b, s: (layer, b, 0, 0)),
        pl.BlockSpec((None, None, N_MEM, C_WIDTH), lambda b, s: (layer, b, 0, 0)),
        lw(D_MODEL, IN_WIDTH), lw(1, IN_WIDTH),
        lw(1, A_WIDTH), lw(1, A_WIDTH),
        lw(A_GROUPS, CHUNK, CHUNK), lw(CHUNK, A_WIDTH),
        lw(CONV_WIDTH, B_WIDTH), lw(1, B_WIDTH),
        lw(B_HEADS, B_HEAD_DIM, 2 * B_HEAD_DIM), lw(2, B_WIDTH), lw(1, B_WIDTH),
        lw(A_WIDTH, D_MODEL), lw(B_WIDTH, D_MODEL), lw(C_WIDTH, D_MODEL),
        lw(D_MODEL, D_MODEL), lw(1, D_MODEL), lw(1, D_MODEL), lw(1, D_MODEL),
    ]
    return pl.pallas_call(
        functools.partial(_mixer_kernel, pre_ln),
        grid=(bsz, seq // tm),
        in_specs=in_specs,
        out_specs=pl.BlockSpec((None, tm, D_MODEL), lambda b, s: (b, s, 0)),
        out_shape=jax.ShapeDtypeStruct(x.shape, F32),
        scratch_shapes=[
            pltpu.VMEM((B_WIDTH // LANES, tm + SUBLANES * SUBLANES, LANES), F32),
            pltpu.VMEM(((CONV_WIDTH - 1) * SUBLANES, B_WIDTH), F32),
            pltpu.VMEM((1, B_WIDTH), F32),
        ],
        compiler_params=pltpu.CompilerParams(
            dimension_semantics=("arbitrary", "arbitrary"), vmem_limit_bytes=VMEM_LIMIT),
    )(x, lni_g, lni_b, kt, vm, w["w_in"], w["b_in"], w["ln_v_g"], w["ln_v_b"], w["w_s"], w["b_s"],
      w["conv_w"], w["conv_b"], w["w_ax"], w["b_ax"], w["lam"], w["p_a"], w["p_b"], w["p_c"],
      w["w_o"], w["b_o"], w["ln1_g"], w["ln1_b"])


def _first_max_of4(c):
    m = jnp.maximum(jnp.maximum(c[0], c[1]), jnp.maximum(c[2], c[3]))
    idx = jnp.where(c[0] == m, 0, jnp.where(c[1] == m, 1, jnp.where(c[2] == m, 2, 3)))
    return m, idx


def _moe_kernel(x_next_ref, x_ref, wr_ref, br_ref, before_ref, wup_ref, wdn_ref, ln2_g_ref, ln2_b_ref,
                out_ref, xh_scr, g2_scr, pt_scr, dest_scr, ends_scr, f_scr):
    tm = x_ref.shape[0]
    step = pl.program_id(0)
    new = lax.rem(step, 2)
    cur = 1 - new

    @pl.when(step == 0)
    def _():
        xh_scr[1] = jnp.zeros(xh_scr.shape[1:], BF16)
        g2_scr[1] = jnp.zeros(g2_scr.shape[1:], BF16)
        pt_scr[1] = jnp.zeros(pt_scr.shape[1:], BF16)
        dest_scr[1] = jnp.zeros(dest_scr.shape[1:], jnp.int32)
        for g in range(N_GROUPS):
            ends_scr[1, g] = 0

    ends = [ends_scr[cur, g] for g in range(N_GROUPS)]
    dest_row = dest_scr[cur, 0:1, :]
    block_row = lax.broadcasted_iota(jnp.int32, (ROW_BLOCK, tm), 0)

    def expert_block(b):
        start = b * ROW_BLOCK
        grp = sum((start >= ends[g]).astype(jnp.int32) for g in range(N_GROUPS - 1))
        perm = jnp.where(block_row + start == dest_row, 1.0, 0.0).astype(BF16)
        xsb = _dot(perm, xh_scr[cur]).astype(BF16)
        gs2 = _dot(perm, g2_scr[cur])
        gsb = gs2[:, :ROUTER_LANES] + gs2[:, ROUTER_LANES:]
        acts = []
        for j in range(EXPERTS_PER_GROUP):
            h = _dot(xsb, wup_ref[grp * EXPERTS_PER_GROUP + j])
            hg = h[:, :D_EXPERT]
            acts.append((_silu(hg) * h[:, D_EXPERT:] * gsb[:, j + 1:j + 2]).astype(BF16))
        f_scr[start:start + ROW_BLOCK, :] = _dot(jnp.concatenate(acts, axis=1), wdn_ref[grp]).astype(BF16)

    for b in range(tm // ROW_BLOCK):
        expert_block(b)

    x_hi = x_next_ref[...].astype(BF16)
    xh_scr[new] = x_hi
    lg = _dot(x_hi, wr_ref[...])
    logits = lg[:, :ROUTER_LANES] + lg[:, ROUTER_LANES:] + br_ref[...]
    lt = logits.T
    neg = -jnp.inf
    g_logit = [lt[g:g + 1, :] for g in range(N_GROUPS)]
    e_logit = [lt[N_GROUPS + e:N_GROUPS + e + 1, :] for e in range(N_EXPERTS)]

    gmax, g_top = _first_max_of4(g_logit)
    p_top = 1.0 / sum(jnp.exp(gl - gmax) for gl in g_logit)

    cand = [jnp.where(g_top == 0, e_logit[j], jnp.where(
        g_top == 1, e_logit[EXPERTS_PER_GROUP + j], jnp.where(
            g_top == 2, e_logit[2 * EXPERTS_PER_GROUP + j], e_logit[3 * EXPERTS_PER_GROUP + j])))
        for j in range(EXPERTS_PER_GROUP)]
    m1, i1 = _first_max_of4(cand)
    m2, i2 = _first_max_of4([jnp.where(i1 == j, neg, cand[j]) for j in range(EXPERTS_PER_GROUP)])
    e2 = jnp.exp(m2 - m1)
    w1 = p_top / (1.0 + e2)
    w2 = w1 * e2
    gates = [jnp.where(i1 == j, w1, 0.0) + jnp.where(i2 == j, w2, 0.0) for j in range(EXPERTS_PER_GROUP)]

    onehot = [jnp.where(g_top == g, 1.0, 0.0) for g in range(N_GROUPS)]
    onehot8 = jnp.concatenate(onehot + [jnp.zeros((SUBLANES - N_GROUPS, tm), F32)], axis=0)
    rank = _dot(onehot8.astype(BF16), before_ref[...])
    dest_f = jnp.zeros((1, tm), F32)
    end = 0
    for g in range(N_GROUPS):
        count = (rank[g:g + 1, tm - 1:tm] + onehot[g][:, tm - 1:tm])[0, 0].astype(jnp.int32)
        base = end.astype(F32) if g else 0.0
        dest_f = dest_f + onehot[g] * (rank[g:g + 1, :] + base)
        n_blocks = lax.shift_right_logical(count + (ROW_BLOCK - 1), ROW_BLOCK.bit_length() - 1)
        end = end + n_blocks * ROW_BLOCK
        ends_scr[new, g] = end
    dest_scr[new] = jnp.broadcast_to(dest_f.astype(jnp.int32), (SUBLANES, tm))

    per_token = jnp.concatenate(
        [dest_f] + gates + [jnp.zeros((LANES - 1 - EXPERTS_PER_GROUP, tm), F32)], axis=0).T
    pt_scr[new] = jnp.where(lax.broadcasted_iota(jnp.int32, (tm, SORTED_ROWS), 1)
                            == per_token[:, 0:1].astype(jnp.int32), 1.0, 0.0).astype(BF16)
    g_hi = per_token.astype(BF16)
    g2_scr[new] = jnp.concatenate([g_hi, (per_token - g_hi.astype(F32)).astype(BF16)], axis=1)

    for b in range(tm // ROW_BLOCK, MAX_BLOCKS):
        used = b * ROW_BLOCK < ends[N_GROUPS - 1]
        pl.when(used)(functools.partial(expert_block, b))

        @pl.when(jnp.logical_not(used))
        def _():
            f_scr[b * ROW_BLOCK:(b + 1) * ROW_BLOCK, :] = jnp.zeros((ROW_BLOCK, D_MODEL), BF16)

    def finish(n_rows):
        for r0 in range(0, tm, FINISH_ROWS):
            rows = slice(r0, r0 + FINISH_ROWS)
            f = _dot(pt_scr[cur, rows, :n_rows], f_scr[:n_rows, :])
            out_ref[rows, :] = _layer_norm(ALPHA * x_ref[rows, :] + f, ln2_g_ref[...], ln2_b_ref[...])

    finish(SORTED_ROWS - ROW_BLOCK)
    pl.when(SORTED_ROWS - ROW_BLOCK < ends[N_GROUPS - 1])(functools.partial(finish, SORTED_ROWS))


def _moe_call(layer, x2d, w):
    n_tok = x2d.shape[0]
    tm = MOE_TILE

    def lw(*shape):
        return _resident((None,) + shape, lambda t: (layer,) + (0,) * len(shape))

    before = jnp.triu(jnp.ones((tm, tm), BF16), 1)
    n_tiles = n_tok // tm
    return pl.pallas_call(
        _moe_kernel,
        grid=(n_tiles + 1,),
        in_specs=[
            pl.BlockSpec((tm, D_MODEL), lambda s: (jnp.minimum(s, n_tiles - 1), 0)),
            pl.BlockSpec((tm, D_MODEL), lambda s: (jnp.maximum(s - 1, 0), 0)),
            lw(D_MODEL, 2 * ROUTER_LANES), lw(1, ROUTER_LANES),
            _resident((tm, tm), lambda s: (0, 0)),
            lw(N_EXPERTS, D_MODEL, 2 * D_EXPERT), lw(N_GROUPS, EXPERTS_PER_GROUP * D_EXPERT, D_MODEL),
            lw(1, D_MODEL), lw(1, D_MODEL),
        ],
        out_specs=pl.BlockSpec((tm, D_MODEL), lambda s: (jnp.maximum(s - 1, 0), 0)),
        out_shape=jax.ShapeDtypeStruct(x2d.shape, F32),
        scratch_shapes=[
            pltpu.VMEM((2, tm, D_MODEL), BF16),
            pltpu.VMEM((2, tm, 2 * ROUTER_LANES), BF16),
            pltpu.VMEM((2, tm, SORTED_ROWS), BF16),
            pltpu.VMEM((2, SUBLANES, tm), jnp.int32),
            pltpu.SMEM((2, N_GROUPS), jnp.int32),
            pltpu.VMEM((SORTED_ROWS, D_MODEL), BF16),
        ],
        compiler_params=pltpu.CompilerParams(
            dimension_semantics=("arbitrary",), vmem_limit_bytes=VMEM_LIMIT),
    )(x2d, x2d, w["wr"], w["br"], before, w["w_up"], w["w_down"], w["ln2_g"], w["ln2_b"])


def kernel(x, mem, ln_in_g, ln_in_b, ln_mem_g, ln_mem_b, w_in, b_in, ln_v_g, ln_v_b, w_s, b_s, conv_w, conv_b, w_a, b_a, w_x, b_x, lam, w_kv, p_a, p_b, p_c, w_o, b_o, ln1_g, ln1_b, w_rg, b_rg, w_re, b_re, w_up, w_down, ln2_g, ln2_b):
    bsz, seq, _ = x.shape
    assert seq % TOKEN_TILE == 0 and TOKEN_TILE % CHUNK == 0 and (bsz * seq) % MOE_TILE == 0

    def row(p):
        return p[:, None, :]

    wr = jnp.concatenate(
        [w_rg, w_re, jnp.zeros((DEPTH, D_MODEL, ROUTER_LANES - N_GROUPS - N_EXPERTS), F32)], axis=-1)
    wr_hi = wr.astype(BF16)
    in_scale = jnp.where(jnp.arange(IN_WIDTH) >= OFF_M, 0.5, 1.0).astype(F32)
    w = {
        "w_in": (w_in * in_scale).astype(BF16), "b_in": row(b_in * in_scale),
        "ln_v_g": row(ln_v_g), "ln_v_b": row(ln_v_b),
        "w_s": w_s.astype(BF16),
        "b_s": jnp.repeat(jnp.swapaxes(b_s, 1, 2), A_WIDTH // A_GROUPS, axis=-1),
        "conv_w": conv_w, "conv_b": row(conv_b),
        "w_ax": (0.5 * jnp.concatenate([w_a, w_x], axis=-1)).astype(BF16),
        "b_ax": 0.5 * jnp.stack([b_a, b_x], axis=1), "lam": row(lam),
        "p_a": p_a.astype(BF16), "p_b": p_b.astype(BF16), "p_c": p_c.astype(BF16),
        "w_o": w_o.astype(BF16), "b_o": row(b_o), "ln1_g": row(ln1_g), "ln1_b": row(ln1_b),
        "wr": jnp.concatenate([wr_hi, (wr - wr_hi.astype(F32)).astype(BF16)], axis=-1),
        "br": row(jnp.concatenate(
            [b_rg, b_re, jnp.zeros((DEPTH, ROUTER_LANES - N_GROUPS - N_EXPERTS), F32)], axis=-1)),
        "w_up": w_up.astype(BF16),
        "w_down": w_down.astype(BF16).reshape(DEPTH, N_GROUPS, EXPERTS_PER_GROUP * D_EXPERT, D_MODEL),
        "ln2_g": row(ln2_g), "ln2_b": row(ln2_b),
    }
    lni_g = ln_in_g.reshape(1, D_MODEL)
    lni_b = ln_in_b.reshape(1, D_MODEL)

    kt, vm = _prep_call(mem, ln_mem_g, ln_mem_b, w_kv.astype(BF16))
    for layer in range(DEPTH):
        x = _mixer_call(layer, layer == 0, x, lni_g, lni_b, kt, vm, w)
        x = _moe_call(layer, x.reshape(bsz * seq, D_MODEL), w).reshape(bsz, seq, D_MODEL)
    return x
```

```python
import functools
import math

import jax
import jax.numpy as jnp
from jax import lax
from jax.experimental import pallas as pl
from jax.experimental.pallas import tpu as pltpu

D_MODEL = 1024
DEPTH = 2
N_MEM = 256
CHUNK = 128
A_GROUPS = 8
A_WIDTH = 1024
B_HEADS = 10
B_WIDTH = 1280
B_HEAD_DIM = B_WIDTH // B_HEADS
CONV_WIDTH = 4
LRU_C = 8.0
C_HEADS = 4
C_HEAD_DIM = 256
C_WIDTH = C_HEADS * C_HEAD_DIM
OFF_U = 0
OFF_V = A_WIDTH
OFF_XB = 2 * A_WIDTH
OFF_GB = OFF_XB + B_WIDTH
SRC_OFF_Q = OFF_GB + B_WIDTH
OFF_M = OFF_GB + B_WIDTH
IN_WIDTH = OFF_M + 3 * D_MODEL
N_GROUPS = 4
EXPERTS_PER_GROUP = 4
N_EXPERTS = 16
D_EXPERT = 256
ALPHA = (2 * DEPTH) ** 0.25
LN_EPS = 1e-5
LOG2_E = 1.0 / math.log(2.0)
GELU_K1 = -2.0 * math.sqrt(2.0 / math.pi) * LOG2_E
GELU_K3 = 0.044715 * GELU_K1

SUBLANES = 8
LANES = 128
ROUTER_LANES = LANES
TOKEN_TILE = 512
MOE_TILE = 512
ROW_BLOCK = 128
MAX_BLOCKS = MOE_TILE // ROW_BLOCK + N_GROUPS
ALWAYS_BLOCKS = MOE_TILE // ROW_BLOCK + 1
SORTED_ROWS = MAX_BLOCKS * ROW_BLOCK
FINISH_ROWS = 256
VMEM_LIMIT = 56 * 1024 * 1024

BF16 = jnp.bfloat16
F32 = jnp.float32


def _dot(a, b):
    return jnp.dot(a, b, preferred_element_type=F32)


def _layer_norm(x, g, b):
    mu = jnp.mean(x, axis=-1, keepdims=True)
    xc = x - mu
    var = jnp.mean(xc * xc, axis=-1, keepdims=True)
    return xc * lax.rsqrt(var + LN_EPS) * g + b


def _gelu(x):
    return x * (1.0 / (1.0 + jnp.exp2(x * (GELU_K1 + GELU_K3 * (x * x)))))


def _sigmoid_of_twice(h):
    return 0.5 + 0.5 * jnp.tanh(h)


def _silu(x):
    return x * (1.0 / (1.0 + jnp.exp2(-LOG2_E * x)))


def _sqrt_nonneg(y):
    return jnp.exp2(jnp.log(y) * (0.5 * LOG2_E))


def _prep_kernel(mem_ref, g_ref, b_ref, wkv_ref, wq_ref, bq_ref, pc_ref, wqk_ref, bqk_ref, vp_ref):
    mem_n = _layer_norm(mem_ref[...], g_ref[...], b_ref[...]).astype(BF16)
    kv = _dot(mem_n, wkv_ref[...])
    kt = kv[:, :C_WIDTH].T.astype(BF16)
    v = kv[:, C_WIDTH:].astype(BF16)
    scale = C_HEAD_DIM ** -0.5
    bq = jnp.broadcast_to(bq_ref[...], (SUBLANES, C_WIDTH))
    bq_hi = bq.astype(BF16)
    bq_lo = (bq - bq_hi.astype(F32)).astype(BF16)
    wqk, bqk, vp = [], [], []
    for h in range(C_HEADS):
        hd = slice(h * C_HEAD_DIM, (h + 1) * C_HEAD_DIM)
        wqk.append((_dot(wq_ref[:, hd], kt[hd, :]) * scale).astype(BF16))
        bqk.append((_dot(bq_hi[:, hd], kt[hd, :]) + _dot(bq_lo[:, hd], kt[hd, :])) * scale)
        vp.append(_dot(v[:, hd], pc_ref[hd, :]).astype(BF16))
    wqk_ref[...] = jnp.concatenate(wqk, axis=1)
    bqk_ref[...] = jnp.concatenate(bqk, axis=1)
    vp_ref[...] = jnp.concatenate(vp, axis=0)


def _prep_call(mem, ln_g, ln_b, w_kv_bf, w_q_bf, b_q, p_c_bf):
    bsz = mem.shape[0]
    att = C_HEADS * N_MEM
    return pl.pallas_call(
        _prep_kernel,
        grid=(DEPTH, bsz),
        in_specs=[
            pl.BlockSpec((None, N_MEM, D_MODEL), lambda l, b: (b, 0, 0)),
            pl.BlockSpec((1, D_MODEL), lambda l, b: (0, 0)),
            pl.BlockSpec((1, D_MODEL), lambda l, b: (0, 0)),
            pl.BlockSpec((None, D_MODEL, 2 * C_WIDTH), lambda l, b: (l, 0, 0)),
            pl.BlockSpec((None, D_MODEL, C_WIDTH), lambda l, b: (l, 0, 0)),
            pl.BlockSpec((None, 1, C_WIDTH), lambda l, b: (l, 0, 0)),
            pl.BlockSpec((None, C_WIDTH, D_MODEL), lambda l, b: (l, 0, 0)),
        ],
        out_specs=[
            pl.BlockSpec((None, None, D_MODEL, att), lambda l, b: (l, b, 0, 0)),
            pl.BlockSpec((None, None, SUBLANES, att), lambda l, b: (l, b, 0, 0)),
            pl.BlockSpec((None, None, att, D_MODEL), lambda l, b: (l, b, 0, 0)),
        ],
        out_shape=[
            jax.ShapeDtypeStruct((DEPTH, bsz, D_MODEL, att), BF16),
            jax.ShapeDtypeStruct((DEPTH, bsz, SUBLANES, att), F32),
            jax.ShapeDtypeStruct((DEPTH, bsz, att, D_MODEL), BF16),
        ],
        compiler_params=pltpu.CompilerParams(
            dimension_semantics=("arbitrary", "arbitrary"), vmem_limit_bytes=VMEM_LIMIT),
    )(mem, ln_g.reshape(1, D_MODEL), ln_b.reshape(1, D_MODEL), w_kv_bf, w_q_bf, b_q, p_c_bf)


def _mixer_kernel(pre_ln, x_ref, lni_g_ref, lni_b_ref, wqk_ref, bqk_ref, vp_ref, w_in_ref, b_in_ref,
                  lnv_g_ref, lnv_b_ref, ws_ref, bs_ref, convw_ref, convb_ref, wax_ref, bax_ref,
                  lam_ref, pa_ref, pb_ref, wo_ref, bo_ref, ln1_g_ref, ln1_b_ref,
                  out_ref, seg_scr, tail_scr, hlast_scr):
    tm = x_ref.shape[0]

    @pl.when(pl.program_id(1) == 0)
    def _():
        tail_scr[...] = jnp.zeros(tail_scr.shape, F32)
        hlast_scr[...] = jnp.zeros((1, B_WIDTH), F32)

    x = x_ref[...]
    if pre_ln:
        x = _layer_norm(x, lni_g_ref[...], lni_b_ref[...])
    xb = x.astype(BF16)

    def proj(lo, width):
        return _dot(xb, w_in_ref[:, lo:lo + width]) + b_in_ref[:, lo:lo + width]

    v_pre = proj(OFF_V, A_WIDTH)
    xbr = proj(OFF_XB, B_WIDTH)
    vn = _layer_norm(_gelu(v_pre), lnv_g_ref[...], lnv_b_ref[...]).astype(BF16)
    u_pre = proj(OFF_U, A_WIDTH)

    seg = tm // SUBLANES
    pitch = seg + SUBLANES
    n_slab = B_WIDTH // LANES
    sub = lax.broadcasted_iota(jnp.int32, (SUBLANES, B_WIDTH), 0)
    for s in range(SUBLANES):
        for l in range(n_slab):
            seg_scr[l, pitch * s:pitch * s + seg, :] = xbr[seg * s:seg * (s + 1), LANES * l:LANES * (l + 1)]
    xs = [jnp.concatenate([seg_scr[l, pl.ds(j, SUBLANES, stride=pitch), :] for l in range(n_slab)], axis=1)
          for j in range(seg)]

    def wrap(cur, prv):
        return pltpu.roll(jnp.where(sub == SUBLANES - 1, prv, cur), 1, 0)

    n_tail = CONV_WIDTH - 1
    ext = [wrap(xs[seg - n_tail + k], tail_scr[SUBLANES * k:SUBLANES * (k + 1), :]) for k in range(n_tail)] + xs
    for k in range(n_tail):
        tail_scr[SUBLANES * k:SUBLANES * (k + 1), :] = xs[seg - n_tail + k]
    conv_w = [jnp.broadcast_to(convw_ref[k:k + 1, :], (SUBLANES, B_WIDTH)) for k in range(CONV_WIDTH)]
    conv_b = jnp.broadcast_to(convb_ref[...], (SUBLANES, B_WIDTH))
    conv_steps = []
    for j in range(seg):
        acc = conv_b
        for k in range(CONV_WIDTH):
            acc = acc + conv_w[k] * ext[j + k]
        conv_steps.append(acc)
    conv = jnp.concatenate(conv_steps, axis=0)
    cb = conv.astype(BF16)
    gb_pre = proj(OFF_GB, B_WIDTH)
    u = _gelu(u_pre)

    row = lax.broadcasted_iota(jnp.int32, (CHUNK, CHUNK), 0)
    col = lax.broadcasted_iota(jnp.int32, (CHUNK, CHUNK), 1)
    causal = col <= row
    ws = [jnp.where(causal, ws_ref[g], jnp.zeros((CHUNK, CHUNK), BF16)) for g in range(A_GROUPS)]
    s_chunks = []
    gd = A_WIDTH // A_GROUPS
    for c in range(0, tm // CHUNK, 2):
        v0 = vn[c * CHUNK:(c + 1) * CHUNK, :]
        v1 = vn[(c + 1) * CHUNK:(c + 2) * CHUNK, :]
        mixed = [_dot(ws[g], jnp.concatenate([v0[:, g * gd:(g + 1) * gd], v1[:, g * gd:(g + 1) * gd]], axis=1))
                 for g in range(A_GROUPS)]
        s_chunks.append(jnp.concatenate([m[:, :gd] for m in mixed], axis=1) + bs_ref[...])
        s_chunks.append(jnp.concatenate([m[:, gd:] for m in mixed], axis=1) + bs_ref[...])
    gate_b = _gelu(gb_pre)
    scores = _dot(xb, wqk_ref[...]) + bqk_ref[0:1, :]
    o_a = (u * jnp.concatenate(s_chunks, axis=0)).astype(BF16)
    ri = jnp.concatenate(
        [_dot(cb[:, h * B_HEAD_DIM:(h + 1) * B_HEAD_DIM], wax_ref[h]) for h in range(B_HEADS)],
        axis=1)
    m_a = proj(OFF_M, D_MODEL)

    r_parts = [ri[:, (2 * h) * B_HEAD_DIM:(2 * h + 1) * B_HEAD_DIM] for h in range(B_HEADS)]
    i_parts = [ri[:, (2 * h + 1) * B_HEAD_DIM:(2 * h + 2) * B_HEAD_DIM] for h in range(B_HEADS)]
    r = _sigmoid_of_twice(jnp.concatenate(r_parts, axis=1) + bax_ref[0:1, :])
    i = _sigmoid_of_twice(jnp.concatenate(i_parts, axis=1) + bax_ref[1:2, :])
    z = -lam_ref[...]
    softplus = jnp.maximum(z, 0.0) + jnp.log(1.0 + jnp.exp(-jnp.abs(z)))
    log_a = (-LRU_C * softplus) * r
    a = jnp.exp(log_a)
    bx = _sqrt_nonneg(1.0 - a * a) * (i * conv)
    a_steps = [a[SUBLANES * j:SUBLANES * (j + 1), :] for j in range(seg)]
    b_steps = [bx[SUBLANES * j:SUBLANES * (j + 1), :] for j in range(seg)]
    y_a = _dot(o_a, pa_ref[...])

    h_loc, p_tot = b_steps[0], a_steps[0]
    for j in range(1, seg):
        h_loc = a_steps[j] * h_loc + b_steps[j]
        p_tot = p_tot * a_steps[j]
    for d in (1, 2, 4):
        m = sub >= d
        h_sh = jnp.where(m, pltpu.roll(h_loc, d, 0), 0.0)
        p_sh = jnp.where(m, pltpu.roll(p_tot, d, 0), 1.0)
        h_loc = p_tot * h_sh + h_loc
        p_tot = p_tot * p_sh
    h_last = hlast_scr[...]
    h_end = h_loc + p_tot * h_last
    hlast_scr[...] = h_end[SUBLANES - 1:SUBLANES, :]
    h = jnp.where(sub == 0, h_last, pltpu.roll(h_end, 1, 0))
    for j in range(seg):
        h = a_steps[j] * h + b_steps[j]
        for l in range(n_slab):
            seg_scr[l, pl.ds(j, SUBLANES, stride=pitch), :] = h[:, LANES * l:LANES * (l + 1)]
    h_tok = jnp.concatenate(
        [jnp.concatenate([seg_scr[l, pitch * s:pitch * s + seg, :] for l in range(n_slab)], axis=1)
         for s in range(SUBLANES)], axis=0)
    o_b = (h_tok * gate_b).astype(BF16)

    m_b = proj(OFF_M + D_MODEL, D_MODEL)
    probs = []
    for h in range(C_HEADS):
        s = scores[:, h * N_MEM:(h + 1) * N_MEM]
        p = jnp.exp(s - jnp.max(s, axis=-1, keepdims=True))
        probs.append((p * (1.0 / jnp.sum(p, axis=-1, keepdims=True))).astype(BF16))
    m_c = proj(OFF_M + 2 * D_MODEL, D_MODEL)
    y_b = _dot(o_b, pb_ref[...])
    y_ab = _sigmoid_of_twice(m_a) * y_a + _sigmoid_of_twice(m_b) * y_b
    y_c = _dot(jnp.concatenate(probs, axis=1), vp_ref[...])

    y = y_ab + _sigmoid_of_twice(m_c) * y_c
    m = _dot(y.astype(BF16), wo_ref[...]) + bo_ref[...]
    out_ref[...] = _layer_norm(ALPHA * x + m, ln1_g_ref[...], ln1_b_ref[...])


def _resident(block_shape, index_map):
    return pl.BlockSpec(block_shape, index_map, pipeline_mode=pl.Buffered(1))


def _mixer_call(layer, pre_ln, x, lni_g, lni_b, wqk, bqk, vp, w):
    bsz, seq, _ = x.shape
    tm = TOKEN_TILE
    att = C_HEADS * N_MEM

    def lw(*shape):
        return _resident((None,) + shape, lambda b, s: (layer,) + (0,) * len(shape))

    in_specs = [
        pl.BlockSpec((None, tm, D_MODEL), lambda b, s: (b, s, 0)),
        _resident((1, D_MODEL), lambda b, s: (0, 0)),
        _resident((1, D_MODEL), lambda b, s: (0, 0)),
        _resident((None, None, D_MODEL, att), lambda b, s: (layer, b, 0, 0)),
        _resident((None, None, SUBLANES, att), lambda b, s: (layer, b, 0, 0)),
        _resident((None, None, att, D_MODEL), lambda b, s: (layer, b, 0, 0)),
        lw(D_MODEL, IN_WIDTH), lw(1, IN_WIDTH),
        lw(1, A_WIDTH), lw(1, A_WIDTH),
        lw(A_GROUPS, CHUNK, CHUNK), lw(CHUNK, A_WIDTH),
        lw(CONV_WIDTH, B_WIDTH), lw(1, B_WIDTH),
        lw(B_HEADS, B_HEAD_DIM, 2 * B_HEAD_DIM), lw(2, B_WIDTH), lw(1, B_WIDTH),
        lw(A_WIDTH, D_MODEL), lw(B_WIDTH, D_MODEL),
        lw(D_MODEL, D_MODEL), lw(1, D_MODEL), lw(1, D_MODEL), lw(1, D_MODEL),
    ]
    return pl.pallas_call(
        functools.partial(_mixer_kernel, pre_ln),
        grid=(bsz, seq // tm),
        in_specs=in_specs,
        out_specs=pl.BlockSpec((None, tm, D_MODEL), lambda b, s: (b, s, 0)),
        out_shape=jax.ShapeDtypeStruct(x.shape, F32),
        scratch_shapes=[
            pltpu.VMEM((B_WIDTH // LANES, tm + SUBLANES * SUBLANES, LANES), F32),
            pltpu.VMEM(((CONV_WIDTH - 1) * SUBLANES, B_WIDTH), F32),
            pltpu.VMEM((1, B_WIDTH), F32),
        ],
        compiler_params=pltpu.CompilerParams(
            dimension_semantics=("arbitrary", "arbitrary"), vmem_limit_bytes=VMEM_LIMIT),
    )(x, lni_g, lni_b, wqk, bqk, vp, w["w_in"], w["b_in"], w["ln_v_g"], w["ln_v_b"], w["w_s"], w["b_s"],
      w["conv_w"], w["conv_b"], w["w_ax"], w["b_ax"], w["lam"], w["p_a"], w["p_b"],
      w["w_o"], w["b_o"], w["ln1_g"], w["ln1_b"])


def _first_max_of4(c):
    m = jnp.maximum(jnp.maximum(c[0], c[1]), jnp.maximum(c[2], c[3]))
    idx = jnp.where(c[0] == m, 0, jnp.where(c[1] == m, 1, jnp.where(c[2] == m, 2, 3)))
    return m, idx


def _moe_kernel(x_next_ref, x_ref, wr_ref, br_ref, before_ref, wup_ref, wdn_ref, ln2_g_ref, ln2_b_ref,
                out_ref, xh_scr, g2_scr, pt_scr, dest_scr, ends_scr, f_scr):
    tm = x_ref.shape[0]
    step = pl.program_id(0)
    new = lax.rem(step, 2)
    cur = 1 - new

    @pl.when(step == 0)
    def _():
        xh_scr[1] = jnp.zeros(xh_scr.shape[1:], BF16)
        g2_scr[1] = jnp.zeros(g2_scr.shape[1:], BF16)
        pt_scr[1] = jnp.zeros(pt_scr.shape[1:], BF16)
        dest_scr[1] = jnp.zeros(dest_scr.shape[1:], jnp.int32)
        for g in range(N_GROUPS):
            ends_scr[1, g] = 0

    ends = [ends_scr[cur, g] for g in range(N_GROUPS)]
    dest_row = dest_scr[cur, 0:1, :]
    block_row = lax.broadcasted_iota(jnp.int32, (ROW_BLOCK, tm), 0)

    def expert_block(b):
        start = b * ROW_BLOCK
        grp = sum((start >= ends[g]).astype(jnp.int32) for g in range(N_GROUPS - 1))
        perm = jnp.where(block_row + start == dest_row, 1.0, 0.0).astype(BF16)
        xsb = _dot(perm, xh_scr[cur]).astype(BF16)
        gs2 = _dot(perm, g2_scr[cur])
        gsb = gs2[:, :ROUTER_LANES] + gs2[:, ROUTER_LANES:]
        acts = []
        for j in range(EXPERTS_PER_GROUP):
            h = _dot(xsb, wup_ref[grp * EXPERTS_PER_GROUP + j])
            hg = h[:, :D_EXPERT]
            acts.append((_silu(hg) * h[:, D_EXPERT:] * gsb[:, j + 1:j + 2]).astype(BF16))
        f_scr[start:start + ROW_BLOCK, :] = _dot(jnp.concatenate(acts, axis=1), wdn_ref[grp]).astype(BF16)

    for b in range(ALWAYS_BLOCKS):
        expert_block(b)

    x_hi = x_next_ref[...].astype(BF16)
    xh_scr[new] = x_hi
    lg = _dot(x_hi, wr_ref[...])
    logits = lg[:, :ROUTER_LANES] + lg[:, ROUTER_LANES:] + br_ref[...]
    lt = logits.T
    neg = -jnp.inf
    g_logit = [lt[g:g + 1, :] for g in range(N_GROUPS)]
    e_logit = [lt[N_GROUPS + e:N_GROUPS + e + 1, :] for e in range(N_EXPERTS)]

    gmax, g_top = _first_max_of4(g_logit)
    p_top = 1.0 / sum(jnp.exp(gl - gmax) for gl in g_logit)

    cand = [jnp.where(g_top == 0, e_logit[j], jnp.where(
        g_top == 1, e_logit[EXPERTS_PER_GROUP + j], jnp.where(
            g_top == 2, e_logit[2 * EXPERTS_PER_GROUP + j], e_logit[3 * EXPERTS_PER_GROUP + j])))
        for j in range(EXPERTS_PER_GROUP)]
    m1, i1 = _first_max_of4(cand)
    m2, i2 = _first_max_of4([jnp.where(i1 == j, neg, cand[j]) for j in range(EXPERTS_PER_GROUP)])
    e2 = jnp.exp(m2 - m1)
    w1 = p_top / (1.0 + e2)
    w2 = w1 * e2
    gates = [jnp.where(i1 == j, w1, 0.0) + jnp.where(i2 == j, w2, 0.0) for j in range(EXPERTS_PER_GROUP)]

    onehot = [jnp.where(g_top == g, 1.0, 0.0) for g in range(N_GROUPS)]
    onehot8 = jnp.concatenate(onehot + [jnp.zeros((SUBLANES - N_GROUPS, tm), F32)], axis=0)
    rank = _dot(onehot8.astype(BF16), before_ref[...])
    dest_f = jnp.zeros((1, tm), F32)
    end = 0
    for g in range(N_GROUPS):
        count = (rank[g:g + 1, tm - 1:tm] + onehot[g][:, tm - 1:tm])[0, 0].astype(jnp.int32)
        base = end.astype(F32) if g else 0.0
        dest_f = dest_f + onehot[g] * (rank[g:g + 1, :] + base)
        n_blocks = lax.shift_right_logical(count + (ROW_BLOCK - 1), ROW_BLOCK.bit_length() - 1)
        end = end + n_blocks * ROW_BLOCK
        ends_scr[new, g] = end
    dest_scr[new] = jnp.broadcast_to(dest_f.astype(jnp.int32), (SUBLANES, tm))

    per_token = jnp.concatenate(
        [dest_f] + gates + [jnp.zeros((LANES - 1 - EXPERTS_PER_GROUP, tm), F32)], axis=0).T
    pt_scr[new] = jnp.where(lax.broadcasted_iota(jnp.int32, (tm, SORTED_ROWS), 1)
                            == per_token[:, 0:1].astype(jnp.int32), 1.0, 0.0).astype(BF16)
    g_hi = per_token.astype(BF16)
    g2_scr[new] = jnp.concatenate([g_hi, (per_token - g_hi.astype(F32)).astype(BF16)], axis=1)

    for b in range(ALWAYS_BLOCKS, MAX_BLOCKS):
        used = b * ROW_BLOCK < ends[N_GROUPS - 1]
        pl.when(used)(functools.partial(expert_block, b))

        @pl.when(jnp.logical_not(used))
        def _():
            f_scr[b * ROW_BLOCK:(b + 1) * ROW_BLOCK, :] = jnp.zeros((ROW_BLOCK, D_MODEL), BF16)

    def finish(n_rows):
        for r0 in range(0, tm, FINISH_ROWS):
            rows = slice(r0, r0 + FINISH_ROWS)
            f = _dot(pt_scr[cur, rows, :n_rows], f_scr[:n_rows, :])
            out_ref[rows, :] = _layer_norm(ALPHA * x_ref[rows, :] + f, ln2_g_ref[...], ln2_b_ref[...])

    finish(SORTED_ROWS - ROW_BLOCK)
    pl.when(SORTED_ROWS - ROW_BLOCK < ends[N_GROUPS - 1])(functools.partial(finish, SORTED_ROWS))


def _moe_call(layer, x2d, w):
    n_tok = x2d.shape[0]
    tm = MOE_TILE

    def lw(*shape):
        return _resident((None,) + shape, lambda t: (layer,) + (0,) * len(shape))

    before = jnp.triu(jnp.ones((tm, tm), BF16), 1)
    n_tiles = n_tok // tm
    return pl.pallas_call(
        _moe_kernel,
        grid=(n_tiles + 1,),
        in_specs=[
            pl.BlockSpec((tm, D_MODEL), lambda s: (jnp.minimum(s, n_tiles - 1), 0)),
            pl.BlockSpec((tm, D_MODEL), lambda s: (jnp.maximum(s - 1, 0), 0)),
            lw(D_MODEL, 2 * ROUTER_LANES), lw(1, ROUTER_LANES),
            _resident((tm, tm), lambda s: (0, 0)),
            lw(N_EXPERTS, D_MODEL, 2 * D_EXPERT), lw(N_GROUPS, EXPERTS_PER_GROUP * D_EXPERT, D_MODEL),
            lw(1, D_MODEL), lw(1, D_MODEL),
        ],
        out_specs=pl.BlockSpec((tm, D_MODEL), lambda s: (jnp.maximum(s - 1, 0), 0)),
        out_shape=jax.ShapeDtypeStruct(x2d.shape, F32),
        scratch_shapes=[
            pltpu.VMEM((2, tm, D_MODEL), BF16),
            pltpu.VMEM((2, tm, 2 * ROUTER_LANES), BF16),
            pltpu.VMEM((2, tm, SORTED_ROWS), BF16),
            pltpu.VMEM((2, SUBLANES, tm), jnp.int32),
            pltpu.SMEM((2, N_GROUPS), jnp.int32),
            pltpu.VMEM((SORTED_ROWS, D_MODEL), BF16),
        ],
        compiler_params=pltpu.CompilerParams(
            dimension_semantics=("arbitrary",), vmem_limit_bytes=VMEM_LIMIT),
    )(x2d, x2d, w["wr"], w["br"], before, w["w_up"], w["w_down"], w["ln2_g"], w["ln2_b"])


def kernel(x, mem, ln_in_g, ln_in_b, ln_mem_g, ln_mem_b, w_in, b_in, ln_v_g, ln_v_b, w_s, b_s, conv_w, conv_b, w_a, b_a, w_x, b_x, lam, w_kv, p_a, p_b, p_c, w_o, b_o, ln1_g, ln1_b, w_rg, b_rg, w_re, b_re, w_up, w_down, ln2_g, ln2_b):
    bsz, seq, _ = x.shape
    assert seq % TOKEN_TILE == 0 and TOKEN_TILE % (2 * CHUNK) == 0 and (bsz * seq) % MOE_TILE == 0

    def row(p):
        return p[:, None, :]

    wr = jnp.concatenate(
        [w_rg, w_re, jnp.zeros((DEPTH, D_MODEL, ROUTER_LANES - N_GROUPS - N_EXPERTS), F32)], axis=-1)
    wr_hi = wr.astype(BF16)
    in_scale = jnp.where(jnp.arange(IN_WIDTH) >= OFF_M, 0.5, 1.0).astype(F32)

    def drop_query(p):
        return jnp.concatenate([p[..., :SRC_OFF_Q], p[..., SRC_OFF_Q + C_WIDTH:]], axis=-1)

    w = {
        "w_in": (drop_query(w_in) * in_scale).astype(BF16), "b_in": row(drop_query(b_in) * in_scale),
        "ln_v_g": row(ln_v_g), "ln_v_b": row(ln_v_b),
        "w_s": w_s.astype(BF16),
        "b_s": jnp.repeat(jnp.swapaxes(b_s, 1, 2), A_WIDTH // A_GROUPS, axis=-1),
        "conv_w": conv_w, "conv_b": row(conv_b),
        "w_ax": (0.5 * jnp.concatenate([w_a, w_x], axis=-1)).astype(BF16),
        "b_ax": 0.5 * jnp.stack([b_a, b_x], axis=1), "lam": row(lam),
        "p_a": p_a.astype(BF16), "p_b": p_b.astype(BF16),
        "w_o": w_o.astype(BF16), "b_o": row(b_o), "ln1_g": row(ln1_g), "ln1_b": row(ln1_b),
        "wr": jnp.concatenate([wr_hi, (wr - wr_hi.astype(F32)).astype(BF16)], axis=-1),
        "br": row(jnp.concatenate(
            [b_rg, b_re, jnp.zeros((DEPTH, ROUTER_LANES - N_GROUPS - N_EXPERTS), F32)], axis=-1)),
        "w_up": w_up.astype(BF16),
        "w_down": w_down.astype(BF16).reshape(DEPTH, N_GROUPS, EXPERTS_PER_GROUP * D_EXPERT, D_MODEL),
        "ln2_g": row(ln2_g), "ln2_b": row(ln2_b),
    }
    lni_g = ln_in_g.reshape(1, D_MODEL)
    lni_b = ln_in_b.reshape(1, D_MODEL)

    wqk, bqk, vp = _prep_call(
        mem, ln_mem_g, ln_mem_b, w_kv.astype(BF16), w_in[..., SRC_OFF_Q:SRC_OFF_Q + C_WIDTH].astype(BF16),
        row(b_in[..., SRC_OFF_Q:SRC_OFF_Q + C_WIDTH]), p_c.astype(BF16))
    for layer in range(DEPTH):
        x = _mixer_call(layer, layer == 0, x, lni_g, lni_b, wqk, bqk, vp, w)
        x = _moe_call(layer, x.reshape(bsz * seq, D_MODEL), w).reshape(bsz, seq, D_MODEL)
    return x
```

```python
import functools
import math

import jax
import jax.numpy as jnp
from jax import lax
from jax.experimental import pallas as pl
from jax.experimental.pallas import tpu as pltpu

D_MODEL = 1024
DEPTH = 2
N_MEM = 256
CHUNK = 128
A_GROUPS = 8
A_WIDTH = 1024
B_HEADS = 10
B_WIDTH = 1280
B_HEAD_DIM = B_WIDTH // B_HEADS
CONV_WIDTH = 4
LRU_C = 8.0
C_HEADS = 4
C_HEAD_DIM = 256
C_WIDTH = C_HEADS * C_HEAD_DIM
IN_WIDTH = 2 * A_WIDTH + 2 * B_WIDTH + C_WIDTH + 3 * D_MODEL
OFF_U = 0
OFF_V = A_WIDTH
OFF_XB = 2 * A_WIDTH
OFF_GB = OFF_XB + B_WIDTH
OFF_Q = OFF_GB + B_WIDTH
OFF_M = OFF_Q + C_WIDTH
N_GROUPS = 4
EXPERTS_PER_GROUP = 4
N_EXPERTS = 16
D_EXPERT = 256
ALPHA = (2 * DEPTH) ** 0.25
LN_EPS = 1e-5
LOG2_E = 1.0 / math.log(2.0)
GELU_K1 = -2.0 * math.sqrt(2.0 / math.pi) * LOG2_E
GELU_K3 = 0.044715 * GELU_K1

SUBLANES = 8
LANES = 128
ROUTER_LANES = LANES
TOKEN_TILE = 512
MOE_TILE = 512
ROW_BLOCK = 128
MAX_BLOCKS = MOE_TILE // ROW_BLOCK + N_GROUPS
ALWAYS_BLOCKS = MOE_TILE // ROW_BLOCK + 1
SORTED_ROWS = MAX_BLOCKS * ROW_BLOCK
FINISH_ROWS = 128
VMEM_LIMIT = 56 * 1024 * 1024

BF16 = jnp.bfloat16
F32 = jnp.float32


def _dot(a, b):
    return jnp.dot(a, b, preferred_element_type=F32)


def _layer_norm(x, g, b):
    mu = jnp.mean(x, axis=-1, keepdims=True)
    xc = x - mu
    var = jnp.mean(xc * xc, axis=-1, keepdims=True)
    return xc * lax.rsqrt(var + LN_EPS) * g + b


def _gelu(x):
    return x * (1.0 / (1.0 + jnp.exp2(x * (GELU_K1 + GELU_K3 * (x * x)))))


def _sigmoid_of_twice(h):
    return 0.5 + 0.5 * jnp.tanh(h)


def _silu(x):
    return x * (1.0 / (1.0 + jnp.exp2(-LOG2_E * x)))


def _sqrt_nonneg(y):
    return jnp.exp2(jnp.log(y) * (0.5 * LOG2_E))


def _prep_kernel(mem_ref, g_ref, b_ref, wkv_ref, kt_ref, v_ref):
    mem_n = _layer_norm(mem_ref[...], g_ref[...], b_ref[...]).astype(BF16)
    kv = _dot(mem_n, wkv_ref[...])
    kt_ref[...] = kv[:, :C_WIDTH].T.astype(BF16)
    v_ref[...] = kv[:, C_WIDTH:].astype(BF16)


def _prep_call(mem, ln_g, ln_b, w_kv_bf):
    bsz = mem.shape[0]
    return pl.pallas_call(
        _prep_kernel,
        grid=(DEPTH, bsz),
        in_specs=[
            pl.BlockSpec((None, N_MEM, D_MODEL), lambda l, b: (b, 0, 0)),
            pl.BlockSpec((1, D_MODEL), lambda l, b: (0, 0)),
            pl.BlockSpec((1, D_MODEL), lambda l, b: (0, 0)),
            pl.BlockSpec((None, D_MODEL, 2 * C_WIDTH), lambda l, b: (l, 0, 0)),
        ],
        out_specs=[
            pl.BlockSpec((None, None, C_WIDTH, N_MEM), lambda l, b: (l, b, 0, 0)),
            pl.BlockSpec((None, None, N_MEM, C_WIDTH), lambda l, b: (l, b, 0, 0)),
        ],
        out_shape=[
            jax.ShapeDtypeStruct((DEPTH, bsz, C_WIDTH, N_MEM), BF16),
            jax.ShapeDtypeStruct((DEPTH, bsz, N_MEM, C_WIDTH), BF16),
        ],
        compiler_params=pltpu.CompilerParams(
            dimension_semantics=("arbitrary", "arbitrary"), vmem_limit_bytes=VMEM_LIMIT),
    )(mem, ln_g.reshape(1, D_MODEL), ln_b.reshape(1, D_MODEL), w_kv_bf)


def _mixer_kernel(pre_ln, x_ref, lni_g_ref, lni_b_ref, kt_ref, vm_ref, w_in_ref, b_in_ref,
                  lnv_g_ref, lnv_b_ref, ws_ref, bs_ref, convw_ref, convb_ref, wax_ref, bax_ref,
                  lam_ref, pa_ref, pb_ref, pc_ref, wo_ref, bo_ref, ln1_g_ref, ln1_b_ref,
                  out_ref, seg_scr, tail_scr, hlast_scr):
    tm = x_ref.shape[0]

    @pl.when(pl.program_id(1) == 0)
    def _():
        tail_scr[...] = jnp.zeros(tail_scr.shape, F32)
        hlast_scr[...] = jnp.zeros((1, B_WIDTH), F32)

    x = x_ref[...]
    if pre_ln:
        x = _layer_norm(x, lni_g_ref[...], lni_b_ref[...])
    xb = x.astype(BF16)

    def proj(lo, width):
        return _dot(xb, w_in_ref[:, lo:lo + width]) + b_in_ref[:, lo:lo + width]

    v_pre = proj(OFF_V, A_WIDTH)
    xbr = proj(OFF_XB, B_WIDTH)
    vn = _layer_norm(_gelu(v_pre), lnv_g_ref[...], lnv_b_ref[...]).astype(BF16)
    u_pre = proj(OFF_U, A_WIDTH)

    seg = tm // SUBLANES
    pitch = seg + SUBLANES
    n_slab = B_WIDTH // LANES
    sub = lax.broadcasted_iota(jnp.int32, (SUBLANES, B_WIDTH), 0)
    for s in range(SUBLANES):
        for l in range(n_slab):
            seg_scr[l, pitch * s:pitch * s + seg, :] = xbr[seg * s:seg * (s + 1), LANES * l:LANES * (l + 1)]
    xs = [jnp.concatenate([seg_scr[l, pl.ds(j, SUBLANES, stride=pitch), :] for l in range(n_slab)], axis=1)
          for j in range(seg)]

    def wrap(cur, prv):
        return pltpu.roll(jnp.where(sub == SUBLANES - 1, prv, cur), 1, 0)

    n_tail = CONV_WIDTH - 1
    ext = [wrap(xs[seg - n_tail + k], tail_scr[SUBLANES * k:SUBLANES * (k + 1), :]) for k in range(n_tail)] + xs
    for k in range(n_tail):
        tail_scr[SUBLANES * k:SUBLANES * (k + 1), :] = xs[seg - n_tail + k]
    conv_w = [jnp.broadcast_to(convw_ref[k:k + 1, :], (SUBLANES, B_WIDTH)) for k in range(CONV_WIDTH)]
    conv_b = jnp.broadcast_to(convb_ref[...], (SUBLANES, B_WIDTH))
    conv_steps = []
    for j in range(seg):
        acc = conv_b
        for k in range(CONV_WIDTH):
            acc = acc + conv_w[k] * ext[j + k]
        conv_steps.append(acc)
    conv = jnp.concatenate(conv_steps, axis=0)
    cb = conv.astype(BF16)
    gb_pre = proj(OFF_GB, B_WIDTH)
    u = _gelu(u_pre)

    row = lax.broadcasted_iota(jnp.int32, (CHUNK, CHUNK), 0)
    col = lax.broadcasted_iota(jnp.int32, (CHUNK, CHUNK), 1)
    causal = col <= row
    ws = [jnp.where(causal, ws_ref[g], jnp.zeros((CHUNK, CHUNK), BF16)) for g in range(A_GROUPS)]
    s_chunks = []
    for c in range(tm // CHUNK):
        vc = vn[c * CHUNK:(c + 1) * CHUNK, :]
        gd = A_WIDTH // A_GROUPS
        s_c = jnp.concatenate(
            [_dot(ws[g], vc[:, g * gd:(g + 1) * gd]) for g in range(A_GROUPS)], axis=1)
        s_chunks.append(s_c + bs_ref[...])
    gate_b = _gelu(gb_pre)
    q = proj(OFF_Q, C_WIDTH).astype(BF16)
    o_a = (u * jnp.concatenate(s_chunks, axis=0)).astype(BF16)
    ri = jnp.concatenate(
        [_dot(cb[:, h * B_HEAD_DIM:(h + 1) * B_HEAD_DIM], wax_ref[h]) for h in range(B_HEADS)],
        axis=1)
    m_a = proj(OFF_M, D_MODEL)

    r_parts = [ri[:, (2 * h) * B_HEAD_DIM:(2 * h + 1) * B_HEAD_DIM] for h in range(B_HEADS)]
    i_parts = [ri[:, (2 * h + 1) * B_HEAD_DIM:(2 * h + 2) * B_HEAD_DIM] for h in range(B_HEADS)]
    r = _sigmoid_of_twice(jnp.concatenate(r_parts, axis=1) + bax_ref[0:1, :])
    i = _sigmoid_of_twice(jnp.concatenate(i_parts, axis=1) + bax_ref[1:2, :])
    z = -lam_ref[...]
    softplus = jnp.maximum(z, 0.0) + jnp.log(1.0 + jnp.exp(-jnp.abs(z)))
    log_a = (-LRU_C * softplus) * r
    a = jnp.exp(log_a)
    bx = _sqrt_nonneg(1.0 - a * a) * (i * conv)
    a_steps = [a[SUBLANES * j:SUBLANES * (j + 1), :] for j in range(seg)]
    b_steps = [bx[SUBLANES * j:SUBLANES * (j + 1), :] for j in range(seg)]
    y_a = _dot(o_a, pa_ref[...])
    scores = [_dot(q[:, h * C_HEAD_DIM:(h + 1) * C_HEAD_DIM], kt_ref[h * C_HEAD_DIM:(h + 1) * C_HEAD_DIM, :])
              for h in range(C_HEADS)]

    h_loc, p_tot = b_steps[0], a_steps[0]
    for j in range(1, seg):
        h_loc = a_steps[j] * h_loc + b_steps[j]
        p_tot = p_tot * a_steps[j]
    for d in (1, 2, 4):
        m = sub >= d
        h_sh = jnp.where(m, pltpu.roll(h_loc, d, 0), 0.0)
        p_sh = jnp.where(m, pltpu.roll(p_tot, d, 0), 1.0)
        h_loc = p_tot * h_sh + h_loc
        p_tot = p_tot * p_sh
    h_last = hlast_scr[...]
    h_end = h_loc + p_tot * h_last
    hlast_scr[...] = h_end[SUBLANES - 1:SUBLANES, :]
    h = jnp.where(sub == 0, h_last, pltpu.roll(h_end, 1, 0))
    for j in range(seg):
        h = a_steps[j] * h + b_steps[j]
        for l in range(n_slab):
            seg_scr[l, pl.ds(j, SUBLANES, stride=pitch), :] = h[:, LANES * l:LANES * (l + 1)]
    h_tok = jnp.concatenate(
        [jnp.concatenate([seg_scr[l, pitch * s:pitch * s + seg, :] for l in range(n_slab)], axis=1)
         for s in range(SUBLANES)], axis=0)
    o_b = (h_tok * gate_b).astype(BF16)

    m_b = proj(OFF_M + D_MODEL, D_MODEL)
    probs, denoms = [], []
    for h in range(C_HEADS):
        s = scores[h] * (C_HEAD_DIM ** -0.5)
        p = jnp.exp(s - jnp.max(s, axis=-1, keepdims=True))
        denoms.append(jnp.sum(p, axis=-1, keepdims=True))
        probs.append(p.astype(BF16))
    m_c = proj(OFF_M + 2 * D_MODEL, D_MODEL)
    y_b = _dot(o_b, pb_ref[...])
    o_c = jnp.concatenate(
        [_dot(probs[h], vm_ref[:, h * C_HEAD_DIM:(h + 1) * C_HEAD_DIM]) / denoms[h] for h in range(C_HEADS)],
        axis=1).astype(BF16)
    y_ab = _sigmoid_of_twice(m_a) * y_a + _sigmoid_of_twice(m_b) * y_b
    y_c = _dot(o_c, pc_ref[...])

    y = y_ab + _sigmoid_of_twice(m_c) * y_c
    m = _dot(y.astype(BF16), wo_ref[...]) + bo_ref[...]
    out_ref[...] = _layer_norm(ALPHA * x + m, ln1_g_ref[...], ln1_b_ref[...])


def _resident(block_shape, index_map):
    return pl.BlockSpec(block_shape, index_map, pipeline_mode=pl.Buffered(1))


def _mixer_call(layer, pre_ln, x, lni_g, lni_b, kt, vm, w):
    bsz, seq, _ = x.shape
    tm = TOKEN_TILE

    def lw(*shape):
        return _resident((None,) + shape, lambda b, s: (layer,) + (0,) * len(shape))

    in_specs = [
        pl.BlockSpec((None, tm, D_MODEL), lambda b, s: (b, s, 0)),
        _resident((1, D_MODEL), lambda b, s: (0, 0)),
        _resident((1, D_MODEL), lambda b, s: (0, 0)),
        pl.BlockSpec((None, None, C_WIDTH, N_MEM), lambda b, s: (layer, b, 0, 0)),
        pl.BlockSpec((None, None, N_MEM, C_WIDTH), lambda b, s: (layer, b, 0, 0)),
        lw(D_MODEL, IN_WIDTH), lw(1, IN_WIDTH),
        lw(1, A_WIDTH), lw(1, A_WIDTH),
        lw(A_GROUPS, CHUNK, CHUNK), lw(CHUNK, A_WIDTH),
        lw(CONV_WIDTH, B_WIDTH), lw(1, B_WIDTH),
        lw(B_HEADS, B_HEAD_DIM, 2 * B_HEAD_DIM), lw(2, B_WIDTH), lw(1, B_WIDTH),
        lw(A_WIDTH, D_MODEL), lw(B_WIDTH, D_MODEL), lw(C_WIDTH, D_MODEL),
        lw(D_MODEL, D_MODEL), lw(1, D_MODEL), lw(1, D_MODEL), lw(1, D_MODEL),
    ]
    return pl.pallas_call(
        functools.partial(_mixer_kernel, pre_ln),
        grid=(bsz, seq // tm),
        in_specs=in_specs,
        out_specs=pl.BlockSpec((None, tm, D_MODEL), lambda b, s: (b, s, 0)),
        out_shape=jax.ShapeDtypeStruct(x.shape, F32),
        scratch_shapes=[
            pltpu.VMEM((B_WIDTH // LANES, tm + SUBLANES * SUBLANES, LANES), F32),
            pltpu.VMEM(((CONV_WIDTH - 1) * SUBLANES, B_WIDTH), F32),
            pltpu.VMEM((1, B_WIDTH), F32),
        ],
        compiler_params=pltpu.CompilerParams(
            dimension_semantics=("arbitrary", "arbitrary"), vmem_limit_bytes=VMEM_LIMIT),
    )(x, lni_g, lni_b, kt, vm, w["w_in"], w["b_in"], w["ln_v_g"], w["ln_v_b"], w["w_s"], w["b_s"],
      w["conv_w"], w["conv_b"], w["w_ax"], w["b_ax"], w["lam"], w["p_a"], w["p_b"], w["p_c"],
      w["w_o"], w["b_o"], w["ln1_g"], w["ln1_b"])


def _first_max_of4(c):
    m = jnp.maximum(jnp.maximum(c[0], c[1]), jnp.maximum(c[2], c[3]))
    idx = jnp.where(c[0] == m, 0, jnp.where(c[1] == m, 1, jnp.where(c[2] == m, 2, 3)))
    return m, idx


def _moe_kernel(x_next_ref, x_ref, wr_ref, br_ref, wup_ref, wdn_ref, ln2_g_ref, ln2_b_ref,
                out_ref, g2_scr, pt_scr, dest_scr, ends_scr, f_scr):
    tm = x_ref.shape[0]
    step = pl.program_id(0)
    new = lax.rem(step, 2)
    cur = 1 - new

    @pl.when(step == 0)
    def _():
        g2_scr[1] = jnp.zeros(g2_scr.shape[1:], BF16)
        pt_scr[1] = jnp.zeros(pt_scr.shape[1:], BF16)
        dest_scr[1] = jnp.zeros(dest_scr.shape[1:], jnp.int32)
        for g in range(N_GROUPS):
            ends_scr[1, g] = 0


    lgt = lax.dot_general(wr_ref[...], x_next_ref[...].astype(BF16), (((1,), (1,)), ((), ())),
                          preferred_element_type=F32)
    lt = lgt[:ROUTER_LANES, :] + lgt[ROUTER_LANES:, :] + br_ref[...]
    neg = -jnp.inf
    g_logit = [lt[g:g + 1, :] for g in range(N_GROUPS)]
    e_logit = [lt[N_GROUPS + e:N_GROUPS + e + 1, :] for e in range(N_EXPERTS)]

    gmax, g_top = _first_max_of4(g_logit)
    p_top = 1.0 / sum(jnp.exp(gl - gmax) for gl in g_logit)

    cand = [jnp.where(g_top == 0, e_logit[j], jnp.where(
        g_top == 1, e_logit[EXPERTS_PER_GROUP + j], jnp.where(
            g_top == 2, e_logit[2 * EXPERTS_PER_GROUP + j], e_logit[3 * EXPERTS_PER_GROUP + j])))
        for j in range(EXPERTS_PER_GROUP)]
    m1, i1 = _first_max_of4(cand)
    m2, i2 = _first_max_of4([jnp.where(i1 == j, neg, cand[j]) for j in range(EXPERTS_PER_GROUP)])
    e2 = jnp.exp(m2 - m1)
    w1 = p_top / (1.0 + e2)
    w2 = w1 * e2
    gates = [jnp.where(i1 == j, w1, 0.0) + jnp.where(i2 == j, w2, 0.0) for j in range(EXPERTS_PER_GROUP)]

    onehot8 = jnp.concatenate([jnp.where(g_top == g, 1.0, 0.0) for g in range(N_GROUPS)]
                              + [jnp.zeros((SUBLANES - N_GROUPS, tm), F32)], axis=0)
    upto = onehot8
    lane = lax.broadcasted_iota(jnp.int32, (SUBLANES, tm), 1)
    shift = 1
    while shift < tm:
        upto = upto + jnp.where(lane >= shift, pltpu.roll(upto, shift, 1), 0.0)
        shift *= 2
    rank = upto - onehot8
    dest_f = jnp.zeros((1, tm), F32)
    new_ends = []
    end = 0
    for g in range(N_GROUPS):
        count = upto[g:g + 1, tm - 1:tm][0, 0].astype(jnp.int32)
        base = end.astype(F32) if g else 0.0
        dest_f = dest_f + onehot8[g:g + 1, :] * (rank[g:g + 1, :] + base)
        n_blocks = lax.shift_right_logical(count + (ROW_BLOCK - 1), ROW_BLOCK.bit_length() - 1)
        end = end + n_blocks * ROW_BLOCK
        new_ends.append(end)
    per_token = jnp.concatenate(
        [dest_f] + gates + [jnp.zeros((LANES - 1 - EXPERTS_PER_GROUP, tm), F32)], axis=0).T

    ends = [ends_scr[cur, g] for g in range(N_GROUPS)]
    dest_row = dest_scr[cur, 0:1, :]
    block_row = lax.broadcasted_iota(jnp.int32, (ROW_BLOCK, tm), 0)

    def expert_block(b):
        start = b * ROW_BLOCK
        grp = sum((start >= ends[g]).astype(jnp.int32) for g in range(N_GROUPS - 1))
        perm = jnp.where(block_row + start == dest_row, 1.0, 0.0).astype(BF16)
        xsb = _dot(perm, x_ref[...].astype(BF16)).astype(BF16)
        gs2 = _dot(perm, g2_scr[cur])
        gsb = gs2[:, :ROUTER_LANES] + gs2[:, ROUTER_LANES:]
        acts = []
        for j in range(EXPERTS_PER_GROUP):
            h = _dot(xsb, wup_ref[grp * EXPERTS_PER_GROUP + j])
            hg = h[:, :D_EXPERT]
            acts.append((_silu(hg) * h[:, D_EXPERT:] * gsb[:, j + 1:j + 2]).astype(BF16))
        f_scr[start:start + ROW_BLOCK, :] = _dot(jnp.concatenate(acts, axis=1), wdn_ref[grp]).astype(BF16)

    for b in range(ALWAYS_BLOCKS):
        expert_block(b)

    for g in range(N_GROUPS):
        ends_scr[new, g] = new_ends[g]
    dest_scr[new] = jnp.broadcast_to(dest_f.astype(jnp.int32), (SUBLANES, tm))
    pt_scr[new] = jnp.where(lax.broadcasted_iota(jnp.int32, (tm, SORTED_ROWS), 1)
                            == per_token[:, 0:1].astype(jnp.int32), 1.0, 0.0).astype(BF16)
    g_hi = per_token.astype(BF16)
    g2_scr[new] = jnp.concatenate([g_hi, (per_token - g_hi.astype(F32)).astype(BF16)], axis=1)

    for b in range(ALWAYS_BLOCKS, MAX_BLOCKS):
        used = b * ROW_BLOCK < ends[N_GROUPS - 1]
        pl.when(used)(functools.partial(expert_block, b))

        @pl.when(jnp.logical_not(used))
        def _():
            f_scr[b * ROW_BLOCK:(b + 1) * ROW_BLOCK, :] = jnp.zeros((ROW_BLOCK, D_MODEL), BF16)

    def finish(n_rows):
        for r0 in range(0, tm, FINISH_ROWS):
            rows = slice(r0, r0 + FINISH_ROWS)
            f = _dot(pt_scr[cur, rows, :n_rows], f_scr[:n_rows, :])
            out_ref[rows, :] = _layer_norm(ALPHA * x_ref[rows, :] + f, ln2_g_ref[...], ln2_b_ref[...])

    finish(SORTED_ROWS - ROW_BLOCK)
    pl.when(SORTED_ROWS - ROW_BLOCK < ends[N_GROUPS - 1])(functools.partial(finish, SORTED_ROWS))


def _moe_call(layer, x2d, w):
    n_tok = x2d.shape[0]
    tm = MOE_TILE

    def lw(*shape):
        return _resident((None,) + shape, lambda t: (layer,) + (0,) * len(shape))

    n_tiles = n_tok // tm
    return pl.pallas_call(
        _moe_kernel,
        grid=(n_tiles + 1,),
        in_specs=[
            pl.BlockSpec((tm, D_MODEL), lambda s: (jnp.minimum(s, n_tiles - 1), 0)),
            pl.BlockSpec((tm, D_MODEL), lambda s: (jnp.maximum(s - 1, 0), 0)),
            lw(2 * ROUTER_LANES, D_MODEL), lw(ROUTER_LANES, 1),
            lw(N_EXPERTS, D_MODEL, 2 * D_EXPERT), lw(N_GROUPS, EXPERTS_PER_GROUP * D_EXPERT, D_MODEL),
            lw(1, D_MODEL), lw(1, D_MODEL),
        ],
        out_specs=pl.BlockSpec((tm, D_MODEL), lambda s: (jnp.maximum(s - 1, 0), 0)),
        out_shape=jax.ShapeDtypeStruct(x2d.shape, F32),
        scratch_shapes=[
            pltpu.VMEM((2, tm, 2 * ROUTER_LANES), BF16),
            pltpu.VMEM((2, tm, SORTED_ROWS), BF16),
            pltpu.VMEM((2, SUBLANES, tm), jnp.int32),
            pltpu.SMEM((2, N_GROUPS), jnp.int32),
            pltpu.VMEM((SORTED_ROWS, D_MODEL), BF16),
        ],
        compiler_params=pltpu.CompilerParams(
            dimension_semantics=("arbitrary",), vmem_limit_bytes=VMEM_LIMIT),
    )(x2d, x2d, w["wr"], w["br"], w["w_up"], w["w_down"], w["ln2_g"], w["ln2_b"])


def kernel(x, mem, ln_in_g, ln_in_b, ln_mem_g, ln_mem_b, w_in, b_in, ln_v_g, ln_v_b, w_s, b_s, conv_w, conv_b, w_a, b_a, w_x, b_x, lam, w_kv, p_a, p_b, p_c, w_o, b_o, ln1_g, ln1_b, w_rg, b_rg, w_re, b_re, w_up, w_down, ln2_g, ln2_b):
    bsz, seq, _ = x.shape
    assert seq % TOKEN_TILE == 0 and TOKEN_TILE % CHUNK == 0 and (bsz * seq) % MOE_TILE == 0

    def row(p):
        return p[:, None, :]

    wr = jnp.concatenate(
        [w_rg, w_re, jnp.zeros((DEPTH, D_MODEL, ROUTER_LANES - N_GROUPS - N_EXPERTS), F32)], axis=-1)
    wr_hi = wr.astype(BF16)
    in_scale = jnp.where(jnp.arange(IN_WIDTH) >= OFF_M, 0.5, 1.0).astype(F32)
    w = {
        "w_in": (w_in * in_scale).astype(BF16), "b_in": row(b_in * in_scale),
        "ln_v_g": row(ln_v_g), "ln_v_b": row(ln_v_b),
        "w_s": w_s.astype(BF16),
        "b_s": jnp.repeat(jnp.swapaxes(b_s, 1, 2), A_WIDTH // A_GROUPS, axis=-1),
        "conv_w": conv_w, "conv_b": row(conv_b),
        "w_ax": (0.5 * jnp.concatenate([w_a, w_x], axis=-1)).astype(BF16),
        "b_ax": 0.5 * jnp.stack([b_a, b_x], axis=1), "lam": row(lam),
        "p_a": p_a.astype(BF16), "p_b": p_b.astype(BF16), "p_c": p_c.astype(BF16),
        "w_o": w_o.astype(BF16), "b_o": row(b_o), "ln1_g": row(ln1_g), "ln1_b": row(ln1_b),
        "wr": jnp.swapaxes(jnp.concatenate([wr_hi, (wr - wr_hi.astype(F32)).astype(BF16)], axis=-1), 1, 2),
        "br": jnp.concatenate(
            [b_rg, b_re, jnp.zeros((DEPTH, ROUTER_LANES - N_GROUPS - N_EXPERTS), F32)], axis=-1)[:, :, None],
        "w_up": w_up.astype(BF16),
        "w_down": w_down.astype(BF16).reshape(DEPTH, N_GROUPS, EXPERTS_PER_GROUP * D_EXPERT, D_MODEL),
        "ln2_g": row(ln2_g), "ln2_b": row(ln2_b),
    }
    lni_g = ln_in_g.reshape(1, D_MODEL)
    lni_b = ln_in_b.reshape(1, D_MODEL)

    kt, vm = _prep_call(mem, ln_mem_g, ln_mem_b, w_kv.astype(BF16))
    for layer in range(DEPTH):
        x = _mixer_call(layer, layer == 0, x, lni_g, lni_b, kt, vm, w)
        x = _moe_call(layer, x.reshape(bsz * seq, D_MODEL), w).reshape(bsz, seq, D_MODEL)
    return x
```

```python
import functools
import math

import jax
import jax.numpy as jnp
from jax import lax
from jax.experimental import pallas as pl
from jax.experimental.pallas import tpu as pltpu

D_MODEL = 1024
DEPTH = 2
N_MEM = 256
CHUNK = 128
A_GROUPS = 8
A_WIDTH = 1024
B_HEADS = 10
B_WIDTH = 1280
B_HEAD_DIM = B_WIDTH // B_HEADS
CONV_WIDTH = 4
LRU_C = 8.0
C_HEADS = 4
C_HEAD_DIM = 256
C_WIDTH = C_HEADS * C_HEAD_DIM
IN_WIDTH = 2 * A_WIDTH + 2 * B_WIDTH + C_WIDTH + 3 * D_MODEL
OFF_U = 0
OFF_V = A_WIDTH
OFF_XB = 2 * A_WIDTH
OFF_GB = OFF_XB + B_WIDTH
OFF_Q = OFF_GB + B_WIDTH
OFF_M = OFF_Q + C_WIDTH
N_GROUPS = 4
EXPERTS_PER_GROUP = 4
N_EXPERTS = 16
D_EXPERT = 256
ALPHA = (2 * DEPTH) ** 0.25
LN_EPS = 1e-5
LOG2_E = 1.0 / math.log(2.0)
GELU_K1 = -2.0 * math.sqrt(2.0 / math.pi) * LOG2_E
GELU_K3 = 0.044715 * GELU_K1

SUBLANES = 8
LANES = 128
ROUTER_LANES = LANES
TOKEN_TILE = 512
MOE_TILE = 512
ROW_BLOCK = 128
MAX_BLOCKS = MOE_TILE // ROW_BLOCK + N_GROUPS
ALWAYS_BLOCKS = MOE_TILE // ROW_BLOCK + 1
SORTED_ROWS = MAX_BLOCKS * ROW_BLOCK
FINISH_ROWS = 128
VMEM_LIMIT = 56 * 1024 * 1024

BF16 = jnp.bfloat16
F32 = jnp.float32


def _dot(a, b):
    return jnp.dot(a, b, preferred_element_type=F32)


def _layer_norm(x, g, b):
    mu = jnp.mean(x, axis=-1, keepdims=True)
    xc = x - mu
    var = jnp.mean(xc * xc, axis=-1, keepdims=True)
    return xc * lax.rsqrt(var + LN_EPS) * g + b


def _gelu(x):
    return x * (1.0 / (1.0 + jnp.exp2(x * (GELU_K1 + GELU_K3 * (x * x)))))


def _sigmoid_of_twice(h):
    return 0.5 + 0.5 * jnp.tanh(h)


def _silu(x):
    return x * (1.0 / (1.0 + jnp.exp2(-LOG2_E * x)))


def _sqrt_nonneg(y):
    return jnp.exp2(jnp.log(y) * (0.5 * LOG2_E))


def _prep_kernel(mem_ref, g_ref, b_ref, wkv_ref, kt_ref, v_ref):
    bsz = mem_ref.shape[0]
    mem = mem_ref[...].reshape(bsz * N_MEM, D_MODEL)
    mem_n = _layer_norm(mem, g_ref[...], b_ref[...]).astype(BF16)
    kv = _dot(mem_n, wkv_ref[...])
    for b in range(bsz):
        rows = slice(b * N_MEM, (b + 1) * N_MEM)
        kt_ref[b] = kv[rows, :C_WIDTH].T.astype(BF16)
        v_ref[b] = kv[rows, C_WIDTH:].astype(BF16)


def _prep_call(mem, ln_g, ln_b, w_kv_bf):
    bsz = mem.shape[0]
    return pl.pallas_call(
        _prep_kernel,
        grid=(DEPTH,),
        in_specs=[
            pl.BlockSpec((bsz, N_MEM, D_MODEL), lambda l: (0, 0, 0)),
            pl.BlockSpec((1, D_MODEL), lambda l: (0, 0)),
            pl.BlockSpec((1, D_MODEL), lambda l: (0, 0)),
            pl.BlockSpec((None, D_MODEL, 2 * C_WIDTH), lambda l: (l, 0, 0)),
        ],
        out_specs=[
            pl.BlockSpec((None, bsz, C_WIDTH, N_MEM), lambda l: (l, 0, 0, 0)),
            pl.BlockSpec((None, bsz, N_MEM, C_WIDTH), lambda l: (l, 0, 0, 0)),
        ],
        out_shape=[
            jax.ShapeDtypeStruct((DEPTH, bsz, C_WIDTH, N_MEM), BF16),
            jax.ShapeDtypeStruct((DEPTH, bsz, N_MEM, C_WIDTH), BF16),
        ],
        compiler_params=pltpu.CompilerParams(
            dimension_semantics=("arbitrary",), vmem_limit_bytes=VMEM_LIMIT),
    )(mem, ln_g.reshape(1, D_MODEL), ln_b.reshape(1, D_MODEL), w_kv_bf)


def _mixer_kernel(layer, x_ref, lni_g_ref, lni_b_ref, kt_ref, vm_ref, w_in_ref, b_in_ref,
                  lnv_g_ref, lnv_b_ref, ws_ref, bs_ref, convw_ref, convb_ref, wax_ref, ba_ref, bx_ref,
                  lam_ref, pa_ref, pb_ref, pc_ref, wo_ref, bo_ref, ln1_g_ref, ln1_b_ref,
                  out_ref, seg_scr, tail_scr, hlast_scr):
    tm = x_ref.shape[0]
    pre_ln = layer == 0
    (b_in_ref, lnv_g_ref, lnv_b_ref, convb_ref, ba_ref, bx_ref, lam_ref, bo_ref, ln1_g_ref, ln1_b_ref) = (
        r.at[layer:layer + 1] for r in (b_in_ref, lnv_g_ref, lnv_b_ref, convb_ref, ba_ref, bx_ref, lam_ref,
                                        bo_ref, ln1_g_ref, ln1_b_ref))
    convw_ref = convw_ref.at[layer]

    @pl.when(pl.program_id(1) == 0)
    def _():
        tail_scr[...] = jnp.zeros(tail_scr.shape, F32)
        hlast_scr[...] = jnp.zeros((1, B_WIDTH), F32)

    x = x_ref[...]
    if pre_ln:
        x = _layer_norm(x, lni_g_ref[...], lni_b_ref[...])
    xb = x.astype(BF16)

    def proj(lo, width):
        bias = b_in_ref[:, lo:lo + width]
        if lo >= OFF_M:
            bias = 0.5 * bias
        return _dot(xb, w_in_ref[:, lo:lo + width]) + bias

    v_pre = proj(OFF_V, A_WIDTH)
    xbr = proj(OFF_XB, B_WIDTH)
    vn = _layer_norm(_gelu(v_pre), lnv_g_ref[...], lnv_b_ref[...]).astype(BF16)
    u_pre = proj(OFF_U, A_WIDTH)

    seg = tm // SUBLANES
    pitch = seg + SUBLANES
    n_slab = B_WIDTH // LANES
    sub = lax.broadcasted_iota(jnp.int32, (SUBLANES, B_WIDTH), 0)
    for s in range(SUBLANES):
        for l in range(n_slab):
            seg_scr[l, pitch * s:pitch * s + seg, :] = xbr[seg * s:seg * (s + 1), LANES * l:LANES * (l + 1)]
    xs = [jnp.concatenate([seg_scr[l, pl.ds(j, SUBLANES, stride=pitch), :] for l in range(n_slab)], axis=1)
          for j in range(seg)]

    def wrap(cur, prv):
        return pltpu.roll(jnp.where(sub == SUBLANES - 1, prv, cur), 1, 0)

    n_tail = CONV_WIDTH - 1
    ext = [wrap(xs[seg - n_tail + k], tail_scr[SUBLANES * k:SUBLANES * (k + 1), :]) for k in range(n_tail)] + xs
    for k in range(n_tail):
        tail_scr[SUBLANES * k:SUBLANES * (k + 1), :] = xs[seg - n_tail + k]
    conv_w = [jnp.broadcast_to(convw_ref[k:k + 1, :], (SUBLANES, B_WIDTH)) for k in range(CONV_WIDTH)]
    conv_b = jnp.broadcast_to(convb_ref[...], (SUBLANES, B_WIDTH))
    conv_steps = []
    for j in range(seg):
        acc = conv_b
        for k in range(CONV_WIDTH):
            acc = acc + conv_w[k] * ext[j + k]
        conv_steps.append(acc)
    conv = jnp.concatenate(conv_steps, axis=0)
    cb = conv.astype(BF16)
    gb_pre = proj(OFF_GB, B_WIDTH)
    u = _gelu(u_pre)

    row = lax.broadcasted_iota(jnp.int32, (CHUNK, CHUNK), 0)
    col = lax.broadcasted_iota(jnp.int32, (CHUNK, CHUNK), 1)
    causal = col <= row
    ws = [jnp.where(causal, ws_ref[g], jnp.zeros((CHUNK, CHUNK), BF16)) for g in range(A_GROUPS)]
    s_chunks = []
    for c in range(tm // CHUNK):
        vc = vn[c * CHUNK:(c + 1) * CHUNK, :]
        gd = A_WIDTH // A_GROUPS
        s_c = jnp.concatenate(
            [_dot(ws[g], vc[:, g * gd:(g + 1) * gd]) for g in range(A_GROUPS)], axis=1)
        s_chunks.append(s_c + bs_ref[...])
    gate_b = _gelu(gb_pre)
    q = proj(OFF_Q, C_WIDTH).astype(BF16)
    o_a = (u * jnp.concatenate(s_chunks, axis=0)).astype(BF16)
    ri = jnp.concatenate(
        [_dot(cb[:, h * B_HEAD_DIM:(h + 1) * B_HEAD_DIM], wax_ref[h]) for h in range(B_HEADS)],
        axis=1)
    m_a = proj(OFF_M, D_MODEL)

    r_parts = [ri[:, (2 * h) * B_HEAD_DIM:(2 * h + 1) * B_HEAD_DIM] for h in range(B_HEADS)]
    i_parts = [ri[:, (2 * h + 1) * B_HEAD_DIM:(2 * h + 2) * B_HEAD_DIM] for h in range(B_HEADS)]
    r = _sigmoid_of_twice(jnp.concatenate(r_parts, axis=1) + 0.5 * ba_ref[...])
    i = _sigmoid_of_twice(jnp.concatenate(i_parts, axis=1) + 0.5 * bx_ref[...])
    z = -lam_ref[...]
    softplus = jnp.maximum(z, 0.0) + jnp.log(1.0 + jnp.exp(-jnp.abs(z)))
    log_a = (-LRU_C * softplus) * r
    a = jnp.exp(log_a)
    bx = _sqrt_nonneg(1.0 - a * a) * (i * conv)
    a_steps = [a[SUBLANES * j:SUBLANES * (j + 1), :] for j in range(seg)]
    b_steps = [bx[SUBLANES * j:SUBLANES * (j + 1), :] for j in range(seg)]
    y_a = _dot(o_a, pa_ref[...])
    scores = [_dot(q[:, h * C_HEAD_DIM:(h + 1) * C_HEAD_DIM], kt_ref[h * C_HEAD_DIM:(h + 1) * C_HEAD_DIM, :])
              for h in range(C_HEADS)]

    h_loc, p_tot = b_steps[0], a_steps[0]
    for j in range(1, seg):
        h_loc = a_steps[j] * h_loc + b_steps[j]
        p_tot = p_tot * a_steps[j]
    for d in (1, 2, 4):
        m = sub >= d
        h_sh = jnp.where(m, pltpu.roll(h_loc, d, 0), 0.0)
        p_sh = jnp.where(m, pltpu.roll(p_tot, d, 0), 1.0)
        h_loc = p_tot * h_sh + h_loc
        p_tot = p_tot * p_sh
    h_last = hlast_scr[...]
    h_end = h_loc + p_tot * h_last
    hlast_scr[...] = h_end[SUBLANES - 1:SUBLANES, :]
    h = jnp.where(sub == 0, h_last, pltpu.roll(h_end, 1, 0))
    for j in range(seg):
        h = a_steps[j] * h + b_steps[j]
        for l in range(n_slab):
            seg_scr[l, pl.ds(j, SUBLANES, stride=pitch), :] = h[:, LANES * l:LANES * (l + 1)]
    h_tok = jnp.concatenate(
        [jnp.concatenate([seg_scr[l, pitch * s:pitch * s + seg, :] for l in range(n_slab)], axis=1)
         for s in range(SUBLANES)], axis=0)
    o_b = (h_tok * gate_b).astype(BF16)

    m_b = proj(OFF_M + D_MODEL, D_MODEL)
    probs, denoms = [], []
    for h in range(C_HEADS):
        s = scores[h] * (C_HEAD_DIM ** -0.5)
        p = jnp.exp(s - jnp.max(s, axis=-1, keepdims=True))
        denoms.append(jnp.sum(p, axis=-1, keepdims=True))
        probs.append(p.astype(BF16))
    m_c = proj(OFF_M + 2 * D_MODEL, D_MODEL)
    y_b = _dot(o_b, pb_ref[...])
    o_c = jnp.concatenate(
        [_dot(probs[h], vm_ref[:, h * C_HEAD_DIM:(h + 1) * C_HEAD_DIM]) / denoms[h] for h in range(C_HEADS)],
        axis=1).astype(BF16)
    y_ab = _sigmoid_of_twice(m_a) * y_a + _sigmoid_of_twice(m_b) * y_b
    y_c = _dot(o_c, pc_ref[...])

    y = y_ab + _sigmoid_of_twice(m_c) * y_c
    m = _dot(y.astype(BF16), wo_ref[...]) + bo_ref[...]
    out_ref[...] = _layer_norm(ALPHA * x + m, ln1_g_ref[...], ln1_b_ref[...])


def _resident(block_shape, index_map):
    return pl.BlockSpec(block_shape, index_map, pipeline_mode=pl.Buffered(1))


def _mixer_call(layer, x, lni_g, lni_b, kt, vm, w):
    bsz, seq, _ = x.shape
    tm = TOKEN_TILE

    def lw(*shape):
        return _resident((None,) + shape, lambda b, s: (layer,) + (0,) * len(shape))

    def vec(*shape):
        return _resident((DEPTH,) + shape, lambda b, s: (0,) * (1 + len(shape)))

    in_specs = [
        pl.BlockSpec((None, tm, D_MODEL), lambda b, s: (b, s, 0)),
        _resident((1, D_MODEL), lambda b, s: (0, 0)),
        _resident((1, D_MODEL), lambda b, s: (0, 0)),
        pl.BlockSpec((None, None, C_WIDTH, N_MEM), lambda b, s: (layer, b, 0, 0)),
        pl.BlockSpec((None, None, N_MEM, C_WIDTH), lambda b, s: (layer, b, 0, 0)),
        lw(D_MODEL, IN_WIDTH), vec(IN_WIDTH),
        vec(A_WIDTH), vec(A_WIDTH),
        lw(A_GROUPS, CHUNK, CHUNK), lw(CHUNK, A_WIDTH),
        vec(CONV_WIDTH, B_WIDTH), vec(B_WIDTH),
        lw(B_HEADS, B_HEAD_DIM, 2 * B_HEAD_DIM), vec(B_WIDTH), vec(B_WIDTH), vec(B_WIDTH),
        lw(A_WIDTH, D_MODEL), lw(B_WIDTH, D_MODEL), lw(C_WIDTH, D_MODEL),
        lw(D_MODEL, D_MODEL), vec(D_MODEL), vec(D_MODEL), vec(D_MODEL),
    ]
    return pl.pallas_call(
        functools.partial(_mixer_kernel, layer),
        grid=(bsz, seq // tm),
        in_specs=in_specs,
        out_specs=pl.BlockSpec((None, tm, D_MODEL), lambda b, s: (b, s, 0)),
        out_shape=jax.ShapeDtypeStruct(x.shape, F32),
        scratch_shapes=[
            pltpu.VMEM((B_WIDTH // LANES, tm + SUBLANES * SUBLANES, LANES), F32),
            pltpu.VMEM(((CONV_WIDTH - 1) * SUBLANES, B_WIDTH), F32),
            pltpu.VMEM((1, B_WIDTH), F32),
        ],
        compiler_params=pltpu.CompilerParams(
            dimension_semantics=("arbitrary", "arbitrary"), vmem_limit_bytes=VMEM_LIMIT),
    )(x, lni_g, lni_b, kt, vm, w["w_in"], w["b_in"], w["ln_v_g"], w["ln_v_b"], w["w_s"], w["b_s"],
      w["conv_w"], w["conv_b"], w["w_ax"], w["b_a"], w["b_x"], w["lam"], w["p_a"], w["p_b"], w["p_c"],
      w["w_o"], w["b_o"], w["ln1_g"], w["ln1_b"])


def _first_max_of4(c):
    m = jnp.maximum(jnp.maximum(c[0], c[1]), jnp.maximum(c[2], c[3]))
    idx = jnp.where(c[0] == m, 0, jnp.where(c[1] == m, 1, jnp.where(c[2] == m, 2, 3)))
    return m, idx


def _moe_kernel(layer, x_next_ref, x_ref, wr_ref, br_ref, wup_ref, wdn_ref, ln2_g_ref, ln2_b_ref,
                out_ref, g2_scr, pt_scr, dest_scr, ends_scr, f_scr):
    tm = x_ref.shape[0]
    step = pl.program_id(0)
    new = lax.rem(step, 2)
    cur = 1 - new
    ln2_g_ref, ln2_b_ref = ln2_g_ref.at[layer:layer + 1], ln2_b_ref.at[layer:layer + 1]

    @pl.when(step == 0)
    def _():
        g2_scr[1] = jnp.zeros(g2_scr.shape[1:], BF16)
        pt_scr[1] = jnp.zeros(pt_scr.shape[1:], BF16)
        dest_scr[1] = jnp.zeros(dest_scr.shape[1:], jnp.int32)
        for g in range(N_GROUPS):
            ends_scr[1, g] = 0


    lgt = lax.dot_general(wr_ref[...], x_next_ref[...].astype(BF16), (((1,), (1,)), ((), ())),
                          preferred_element_type=F32)
    lt = lgt[:ROUTER_LANES, :] + lgt[ROUTER_LANES:, :] + br_ref[...]
    neg = -jnp.inf
    g_logit = [lt[g:g + 1, :] for g in range(N_GROUPS)]
    e_logit = [lt[N_GROUPS + e:N_GROUPS + e + 1, :] for e in range(N_EXPERTS)]

    gmax, g_top = _first_max_of4(g_logit)
    p_top = 1.0 / sum(jnp.exp(gl - gmax) for gl in g_logit)

    cand = [jnp.where(g_top == 0, e_logit[j], jnp.where(
        g_top == 1, e_logit[EXPERTS_PER_GROUP + j], jnp.where(
            g_top == 2, e_logit[2 * EXPERTS_PER_GROUP + j], e_logit[3 * EXPERTS_PER_GROUP + j])))
        for j in range(EXPERTS_PER_GROUP)]
    m1, i1 = _first_max_of4(cand)
    m2, i2 = _first_max_of4([jnp.where(i1 == j, neg, cand[j]) for j in range(EXPERTS_PER_GROUP)])
    e2 = jnp.exp(m2 - m1)
    w1 = p_top / (1.0 + e2)
    w2 = w1 * e2
    gates = [jnp.where(i1 == j, w1, 0.0) + jnp.where(i2 == j, w2, 0.0) for j in range(EXPERTS_PER_GROUP)]

    onehot8 = jnp.concatenate([jnp.where(g_top == g, 1.0, 0.0) for g in range(N_GROUPS)]
                              + [jnp.zeros((SUBLANES - N_GROUPS, tm), F32)], axis=0)
    upto = onehot8
    lane = lax.broadcasted_iota(jnp.int32, (SUBLANES, tm), 1)
    shift = 1
    while shift < tm:
        upto = upto + jnp.where(lane >= shift, pltpu.roll(upto, shift, 1), 0.0)
        shift *= 2
    rank = upto - onehot8
    dest_f = jnp.zeros((1, tm), F32)
    new_ends = []
    end = 0
    for g in range(N_GROUPS):
        count = upto[g:g + 1, tm - 1:tm][0, 0].astype(jnp.int32)
        base = end.astype(F32) if g else 0.0
        dest_f = dest_f + onehot8[g:g + 1, :] * (rank[g:g + 1, :] + base)
        n_blocks = lax.shift_right_logical(count + (ROW_BLOCK - 1), ROW_BLOCK.bit_length() - 1)
        end = end + n_blocks * ROW_BLOCK
        new_ends.append(end)
    per_token = jnp.concatenate(
        [dest_f] + gates + [jnp.zeros((LANES - 1 - EXPERTS_PER_GROUP, tm), F32)], axis=0).T

    ends = [ends_scr[cur, g] for g in range(N_GROUPS)]
    dest_row = dest_scr[cur, 0:1, :]
    block_row = lax.broadcasted_iota(jnp.int32, (ROW_BLOCK, tm), 0)

    def expert_block(b):
        start = b * ROW_BLOCK
        grp = sum((start >= ends[g]).astype(jnp.int32) for g in range(N_GROUPS - 1))
        perm = jnp.where(block_row + start == dest_row, 1.0, 0.0).astype(BF16)
        xsb = _dot(perm, x_ref[...].astype(BF16)).astype(BF16)
        gs2 = _dot(perm, g2_scr[cur])
        gsb = gs2[:, :ROUTER_LANES] + gs2[:, ROUTER_LANES:]
        acts = []
        for j in range(EXPERTS_PER_GROUP):
            h = _dot(xsb, wup_ref[grp * EXPERTS_PER_GROUP + j])
            hg = h[:, :D_EXPERT]
            acts.append((_silu(hg) * h[:, D_EXPERT:] * gsb[:, j + 1:j + 2]).astype(BF16))
        f_scr[start:start + ROW_BLOCK, :] = _dot(jnp.concatenate(acts, axis=1), wdn_ref[grp]).astype(BF16)

    for b in range(ALWAYS_BLOCKS):
        expert_block(b)

    for g in range(N_GROUPS):
        ends_scr[new, g] = new_ends[g]
    dest_scr[new] = jnp.broadcast_to(dest_f.astype(jnp.int32), (SUBLANES, tm))
    pt_scr[new] = jnp.where(lax.broadcasted_iota(jnp.int32, (tm, SORTED_ROWS), 1)
                            == per_token[:, 0:1].astype(jnp.int32), 1.0, 0.0).astype(BF16)
    g_hi = per_token.astype(BF16)
    g2_scr[new] = jnp.concatenate([g_hi, (per_token - g_hi.astype(F32)).astype(BF16)], axis=1)

    for b in range(ALWAYS_BLOCKS, MAX_BLOCKS):
        used = b * ROW_BLOCK < ends[N_GROUPS - 1]
        pl.when(used)(functools.partial(expert_block, b))

        @pl.when(jnp.logical_not(used))
        def _():
            f_scr[b * ROW_BLOCK:(b + 1) * ROW_BLOCK, :] = jnp.zeros((ROW_BLOCK, D_MODEL), BF16)

    def finish(n_rows):
        for r0 in range(0, tm, FINISH_ROWS):
            rows = slice(r0, r0 + FINISH_ROWS)
            f = _dot(pt_scr[cur, rows, :n_rows], f_scr[:n_rows, :])
            out_ref[rows, :] = _layer_norm(ALPHA * x_ref[rows, :] + f, ln2_g_ref[...], ln2_b_ref[...])

    finish(SORTED_ROWS - ROW_BLOCK)
    pl.when(SORTED_ROWS - ROW_BLOCK < ends[N_GROUPS - 1])(functools.partial(finish, SORTED_ROWS))


def _moe_call(layer, x2d, w):
    n_tok = x2d.shape[0]
    tm = MOE_TILE

    def lw(*shape):
        return _resident((None,) + shape, lambda t: (layer,) + (0,) * len(shape))

    n_tiles = n_tok // tm
    return pl.pallas_call(
        functools.partial(_moe_kernel, layer),
        grid=(n_tiles + 1,),
        in_specs=[
            pl.BlockSpec((tm, D_MODEL), lambda s: (jnp.minimum(s, n_tiles - 1), 0)),
            pl.BlockSpec((tm, D_MODEL), lambda s: (jnp.maximum(s - 1, 0), 0)),
            lw(2 * ROUTER_LANES, D_MODEL), lw(ROUTER_LANES, 1),
            lw(N_EXPERTS, D_MODEL, 2 * D_EXPERT), lw(N_GROUPS, EXPERTS_PER_GROUP * D_EXPERT, D_MODEL),
            _resident((DEPTH, D_MODEL), lambda s: (0, 0)), _resident((DEPTH, D_MODEL), lambda s: (0, 0)),
        ],
        out_specs=pl.BlockSpec((tm, D_MODEL), lambda s: (jnp.maximum(s - 1, 0), 0)),
        out_shape=jax.ShapeDtypeStruct(x2d.shape, F32),
        scratch_shapes=[
            pltpu.VMEM((2, tm, 2 * ROUTER_LANES), BF16),
            pltpu.VMEM((2, tm, SORTED_ROWS), BF16),
            pltpu.VMEM((2, SUBLANES, tm), jnp.int32),
            pltpu.SMEM((2, N_GROUPS), jnp.int32),
            pltpu.VMEM((SORTED_ROWS, D_MODEL), BF16),
        ],
        compiler_params=pltpu.CompilerParams(
            dimension_semantics=("arbitrary",), vmem_limit_bytes=VMEM_LIMIT),
    )(x2d, x2d, w["wr"], w["br"], w["w_up"], w["w_down"], w["ln2_g"], w["ln2_b"])


def kernel(x, mem, ln_in_g, ln_in_b, ln_mem_g, ln_mem_b, w_in, b_in, ln_v_g, ln_v_b, w_s, b_s, conv_w, conv_b, w_a, b_a, w_x, b_x, lam, w_kv, p_a, p_b, p_c, w_o, b_o, ln1_g, ln1_b, w_rg, b_rg, w_re, b_re, w_up, w_down, ln2_g, ln2_b):
    bsz, seq, _ = x.shape
    assert seq % TOKEN_TILE == 0 and TOKEN_TILE % CHUNK == 0 and (bsz * seq) % MOE_TILE == 0

    wr = jnp.concatenate(
        [w_rg, w_re, jnp.zeros((DEPTH, D_MODEL, ROUTER_LANES - N_GROUPS - N_EXPERTS), F32)], axis=-1)
    wr_hi = wr.astype(BF16)
    in_scale = jnp.where(jnp.arange(IN_WIDTH) >= OFF_M, 0.5, 1.0).astype(F32)
    w = {
        "w_in": (w_in * in_scale).astype(BF16), "b_in": b_in,
        "ln_v_g": ln_v_g, "ln_v_b": ln_v_b,
        "w_s": w_s.astype(BF16),
        "b_s": jnp.repeat(jnp.swapaxes(b_s, 1, 2), A_WIDTH // A_GROUPS, axis=-1),
        "conv_w": conv_w, "conv_b": conv_b,
        "w_ax": (0.5 * jnp.concatenate([w_a, w_x], axis=-1)).astype(BF16),
        "b_a": b_a, "b_x": b_x, "lam": lam,
        "p_a": p_a.astype(BF16), "p_b": p_b.astype(BF16), "p_c": p_c.astype(BF16),
        "w_o": w_o.astype(BF16), "b_o": b_o, "ln1_g": ln1_g, "ln1_b": ln1_b,
        "wr": jnp.swapaxes(jnp.concatenate([wr_hi, (wr - wr_hi.astype(F32)).astype(BF16)], axis=-1), 1, 2),
        "br": jnp.concatenate(
            [b_rg, b_re, jnp.zeros((DEPTH, ROUTER_LANES - N_GROUPS - N_EXPERTS), F32)], axis=-1)[:, :, None],
        "w_up": w_up.astype(BF16),
        "w_down": w_down.astype(BF16).reshape(DEPTH, N_GROUPS, EXPERTS_PER_GROUP * D_EXPERT, D_MODEL),
        "ln2_g": ln2_g, "ln2_b": ln2_b,
    }
    lni_g = ln_in_g.reshape(1, D_MODEL)
    lni_b = ln_in_b.reshape(1, D_MODEL)

    kt, vm = _prep_call(mem, ln_mem_g, ln_mem_b, w_kv.astype(BF16))
    for layer in range(DEPTH):
        x = _mixer_call(layer, x, lni_g, lni_b, kt, vm, w)
        x = _moe_call(layer, x.reshape(bsz * seq, D_MODEL), w).reshape(bsz, seq, D_MODEL)
    return x
```

```python
import functools
import math

import jax
import jax.numpy as jnp
from jax import lax
from jax.experimental import pallas as pl
from jax.experimental.pallas import tpu as pltpu

D_MODEL = 1024
DEPTH = 2
N_MEM = 256
CHUNK = 128
A_GROUPS = 8
A_WIDTH = 1024
B_HEADS = 10
B_WIDTH = 1280
B_HEAD_DIM = B_WIDTH // B_HEADS
CONV_WIDTH = 4
LRU_C = 8.0
C_HEADS = 4
C_HEAD_DIM = 256
C_WIDTH = C_HEADS * C_HEAD_DIM
IN_WIDTH = 2 * A_WIDTH + 2 * B_WIDTH + C_WIDTH + 3 * D_MODEL
OFF_U = 0
OFF_V = A_WIDTH
OFF_XB = 2 * A_WIDTH
OFF_GB = OFF_XB + B_WIDTH
OFF_Q = OFF_GB + B_WIDTH
OFF_M = OFF_Q + C_WIDTH
N_GROUPS = 4
EXPERTS_PER_GROUP = 4
N_EXPERTS = 16
D_EXPERT = 256
ALPHA = (2 * DEPTH) ** 0.25
LN_EPS = 1e-5
LOG2_E = 1.0 / math.log(2.0)
GELU_K1 = -2.0 * math.sqrt(2.0 / math.pi) * LOG2_E
GELU_K3 = 0.044715 * GELU_K1

SUBLANES = 8
LANES = 128
ROUTER_LANES = LANES
TOKEN_TILE = 512
MOE_TILE = 512
ROW_BLOCK = 128
MAX_BLOCKS = MOE_TILE // ROW_BLOCK + N_GROUPS
ALWAYS_BLOCKS = MOE_TILE // ROW_BLOCK + 1
SORTED_ROWS = MAX_BLOCKS * ROW_BLOCK
FINISH_ROWS = 128
VMEM_LIMIT = 56 * 1024 * 1024

BF16 = jnp.bfloat16
F32 = jnp.float32


def _dot(a, b):
    return jnp.dot(a, b, preferred_element_type=F32)


def _layer_norm(x, g, b):
    mu = jnp.mean(x, axis=-1, keepdims=True)
    xc = x - mu
    var = jnp.mean(xc * xc, axis=-1, keepdims=True)
    return xc * lax.rsqrt(var + LN_EPS) * g + b


def _gelu(x):
    return x * (1.0 / (1.0 + jnp.exp2(x * (GELU_K1 + GELU_K3 * (x * x)))))


def _sigmoid_of_twice(h):
    return 0.5 + 0.5 * jnp.tanh(h)


def _silu(x):
    return x * (1.0 / (1.0 + jnp.exp2(-LOG2_E * x)))


def _sqrt_nonneg(y):
    return jnp.exp2(jnp.log(y) * (0.5 * LOG2_E))


def _prep_kernel(mem_ref, g_ref, b_ref, wkv_ref, kt_ref, v_ref):
    bsz = mem_ref.shape[0]
    mem = mem_ref[...].reshape(bsz * N_MEM, D_MODEL)
    mem_n = _layer_norm(mem, g_ref[...], b_ref[...]).astype(BF16)
    kv = _dot(mem_n, wkv_ref[...])
    for b in range(bsz):
        rows = slice(b * N_MEM, (b + 1) * N_MEM)
        kt_ref[b] = kv[rows, :C_WIDTH].T.astype(BF16)
        v_ref[b] = kv[rows, C_WIDTH:].astype(BF16)


def _prep_call(mem, ln_g, ln_b, w_kv_bf):
    bsz = mem.shape[0]
    return pl.pallas_call(
        _prep_kernel,
        grid=(DEPTH,),
        in_specs=[
            pl.BlockSpec((bsz, N_MEM, D_MODEL), lambda l: (0, 0, 0)),
            pl.BlockSpec((1, D_MODEL), lambda l: (0, 0)),
            pl.BlockSpec((1, D_MODEL), lambda l: (0, 0)),
            pl.BlockSpec((None, D_MODEL, 2 * C_WIDTH), lambda l: (l, 0, 0)),
        ],
        out_specs=[
            pl.BlockSpec((None, bsz, C_WIDTH, N_MEM), lambda l: (l, 0, 0, 0)),
            pl.BlockSpec((None, bsz, N_MEM, C_WIDTH), lambda l: (l, 0, 0, 0)),
        ],
        out_shape=[
            jax.ShapeDtypeStruct((DEPTH, bsz, C_WIDTH, N_MEM), BF16),
            jax.ShapeDtypeStruct((DEPTH, bsz, N_MEM, C_WIDTH), BF16),
        ],
        compiler_params=pltpu.CompilerParams(
            dimension_semantics=("arbitrary",), vmem_limit_bytes=VMEM_LIMIT),
    )(mem, ln_g.reshape(1, D_MODEL), ln_b.reshape(1, D_MODEL), w_kv_bf)


def _mixer_kernel(layer, x_ref, lni_g_ref, lni_b_ref, kt_ref, vm_ref, w_in_ref, b_in_ref,
                  lnv_g_ref, lnv_b_ref, ws_ref, bs_ref, convw_ref, convb_ref, wax_ref, ba_ref, bx_ref,
                  lam_ref, pa_ref, pb_ref, pc_ref, wo_ref, bo_ref, ln1_g_ref, ln1_b_ref,
                  out_ref, seg_scr, tail_scr, hlast_scr):
    tm = x_ref.shape[0]
    pre_ln = layer == 0
    (b_in_ref, lnv_g_ref, lnv_b_ref, convb_ref, ba_ref, bx_ref, lam_ref, bo_ref, ln1_g_ref, ln1_b_ref) = (
        r.at[layer:layer + 1] for r in (b_in_ref, lnv_g_ref, lnv_b_ref, convb_ref, ba_ref, bx_ref, lam_ref,
                                        bo_ref, ln1_g_ref, ln1_b_ref))
    convw_ref = convw_ref.at[layer]

    @pl.when(pl.program_id(1) == 0)
    def _():
        tail_scr[...] = jnp.zeros(tail_scr.shape, F32)
        hlast_scr[...] = jnp.zeros((1, B_WIDTH), F32)

    x = x_ref[...]
    if pre_ln:
        x = _layer_norm(x, lni_g_ref[...], lni_b_ref[...])
    xb = x.astype(BF16)

    def proj(lo, width):
        bias = b_in_ref[:, lo:lo + width]
        if lo >= OFF_M:
            bias = 0.5 * bias
        return _dot(xb, w_in_ref[:, lo:lo + width]) + bias

    v_pre = proj(OFF_V, A_WIDTH)
    xbr = proj(OFF_XB, B_WIDTH)
    vn = _layer_norm(_gelu(v_pre), lnv_g_ref[...], lnv_b_ref[...]).astype(BF16)
    u_pre = proj(OFF_U, A_WIDTH)

    seg = tm // SUBLANES
    pitch = seg + SUBLANES
    n_slab = B_WIDTH // LANES
    sub = lax.broadcasted_iota(jnp.int32, (SUBLANES, B_WIDTH), 0)
    for s in range(SUBLANES):
        for l in range(n_slab):
            seg_scr[l, pitch * s:pitch * s + seg, :] = xbr[seg * s:seg * (s + 1), LANES * l:LANES * (l + 1)]
    xs = [jnp.concatenate([seg_scr[l, pl.ds(j, SUBLANES, stride=pitch), :] for l in range(n_slab)], axis=1)
          for j in range(seg)]

    def wrap(cur, prv):
        return pltpu.roll(jnp.where(sub == SUBLANES - 1, prv, cur), 1, 0)

    n_tail = CONV_WIDTH - 1
    ext = [wrap(xs[seg - n_tail + k], tail_scr[SUBLANES * k:SUBLANES * (k + 1), :]) for k in range(n_tail)] + xs
    for k in range(n_tail):
        tail_scr[SUBLANES * k:SUBLANES * (k + 1), :] = xs[seg - n_tail + k]
    conv_w = [jnp.broadcast_to(convw_ref[k:k + 1, :], (SUBLANES, B_WIDTH)) for k in range(CONV_WIDTH)]
    conv_b = jnp.broadcast_to(convb_ref[...], (SUBLANES, B_WIDTH))
    conv_steps = []
    for j in range(seg):
        acc = conv_b
        for k in range(CONV_WIDTH):
            acc = acc + conv_w[k] * ext[j + k]
        conv_steps.append(acc)
    conv = jnp.concatenate(conv_steps, axis=0)
    cb = conv.astype(BF16)
    gb_pre = proj(OFF_GB, B_WIDTH)
    u = _gelu(u_pre)

    row = lax.broadcasted_iota(jnp.int32, (CHUNK, CHUNK), 0)
    col = lax.broadcasted_iota(jnp.int32, (CHUNK, CHUNK), 1)
    causal = col <= row
    ws = [jnp.where(causal, ws_ref[g], jnp.zeros((CHUNK, CHUNK), BF16)) for g in range(A_GROUPS)]
    s_chunks = []
    for c in range(tm // CHUNK):
        vc = vn[c * CHUNK:(c + 1) * CHUNK, :]
        gd = A_WIDTH // A_GROUPS
        s_c = jnp.concatenate(
            [_dot(ws[g], vc[:, g * gd:(g + 1) * gd]) for g in range(A_GROUPS)], axis=1)
        s_chunks.append(s_c + bs_ref[...])
    gate_b = _gelu(gb_pre)
    q = proj(OFF_Q, C_WIDTH).astype(BF16)
    o_a = (u * jnp.concatenate(s_chunks, axis=0)).astype(BF16)
    ri = jnp.concatenate(
        [_dot(cb[:, h * B_HEAD_DIM:(h + 1) * B_HEAD_DIM], wax_ref[h]) for h in range(B_HEADS)],
        axis=1)
    m_a = proj(OFF_M, D_MODEL)

    r_parts = [ri[:, (2 * h) * B_HEAD_DIM:(2 * h + 1) * B_HEAD_DIM] for h in range(B_HEADS)]
    i_parts = [ri[:, (2 * h + 1) * B_HEAD_DIM:(2 * h + 2) * B_HEAD_DIM] for h in range(B_HEADS)]
    r = _sigmoid_of_twice(jnp.concatenate(r_parts, axis=1) + 0.5 * ba_ref[...])
    i = _sigmoid_of_twice(jnp.concatenate(i_parts, axis=1) + 0.5 * bx_ref[...])
    z = -lam_ref[...]
    softplus = jnp.maximum(z, 0.0) + jnp.log(1.0 + jnp.exp(-jnp.abs(z)))
    log_a = (-LRU_C * softplus) * r
    a = jnp.exp(log_a)
    bx = _sqrt_nonneg(1.0 - a * a) * (i * conv)
    a_steps = [a[SUBLANES * j:SUBLANES * (j + 1), :] for j in range(seg)]
    b_steps = [bx[SUBLANES * j:SUBLANES * (j + 1), :] for j in range(seg)]
    y_a = _dot(o_a, pa_ref[...])
    scores = [_dot(q[:, h * C_HEAD_DIM:(h + 1) * C_HEAD_DIM], kt_ref[h * C_HEAD_DIM:(h + 1) * C_HEAD_DIM, :])
              for h in range(C_HEADS)]

    h_loc, p_tot = b_steps[0], a_steps[0]
    for j in range(1, seg):
        h_loc = a_steps[j] * h_loc + b_steps[j]
        p_tot = p_tot * a_steps[j]
    for d in (1, 2, 4):
        m = sub >= d
        h_sh = jnp.where(m, pltpu.roll(h_loc, d, 0), 0.0)
        p_sh = jnp.where(m, pltpu.roll(p_tot, d, 0), 1.0)
        h_loc = p_tot * h_sh + h_loc
        p_tot = p_tot * p_sh
    h_last = hlast_scr[...]
    h_end = h_loc + p_tot * h_last
    hlast_scr[...] = h_end[SUBLANES - 1:SUBLANES, :]
    h = jnp.where(sub == 0, h_last, pltpu.roll(h_end, 1, 0))
    for j in range(seg):
        h = a_steps[j] * h + b_steps[j]
        for l in range(n_slab):
            seg_scr[l, pl.ds(j, SUBLANES, stride=pitch), :] = h[:, LANES * l:LANES * (l + 1)]
    h_tok = jnp.concatenate(
        [jnp.concatenate([seg_scr[l, pitch * s:pitch * s + seg, :] for l in range(n_slab)], axis=1)
         for s in range(SUBLANES)], axis=0)
    o_b = (h_tok * gate_b).astype(BF16)

    m_b = proj(OFF_M + D_MODEL, D_MODEL)
    probs, denoms = [], []
    for h in range(C_HEADS):
        s = scores[h] * (C_HEAD_DIM ** -0.5)
        p = jnp.exp(s - jnp.max(s, axis=-1, keepdims=True))
        denoms.append(jnp.sum(p, axis=-1, keepdims=True))
        probs.append(p.astype(BF16))
    m_c = proj(OFF_M + 2 * D_MODEL, D_MODEL)
    y_b = _dot(o_b, pb_ref[...])
    o_c = jnp.concatenate(
        [_dot(probs[h], vm_ref[:, h * C_HEAD_DIM:(h + 1) * C_HEAD_DIM]) / denoms[h] for h in range(C_HEADS)],
        axis=1).astype(BF16)
    y_ab = _sigmoid_of_twice(m_a) * y_a + _sigmoid_of_twice(m_b) * y_b
    y_c = _dot(o_c, pc_ref[...])

    y = y_ab + _sigmoid_of_twice(m_c) * y_c
    m = _dot(y.astype(BF16), wo_ref[...]) + bo_ref[...]
    out_ref[...] = _layer_norm(ALPHA * x + m, ln1_g_ref[...], ln1_b_ref[...])


def _resident(block_shape, index_map):
    return pl.BlockSpec(block_shape, index_map, pipeline_mode=pl.Buffered(1))


def _mixer_call(layer, x, lni_g, lni_b, kt, vm, w):
    bsz, seq, _ = x.shape
    tm = TOKEN_TILE

    def lw(*shape):
        return _resident((None,) + shape, lambda b, s: (layer,) + (0,) * len(shape))

    def vec(*shape):
        return _resident((DEPTH,) + shape, lambda b, s: (0,) * (1 + len(shape)))

    in_specs = [
        pl.BlockSpec((None, tm, D_MODEL), lambda b, s: (b, s, 0)),
        _resident((1, D_MODEL), lambda b, s: (0, 0)),
        _resident((1, D_MODEL), lambda b, s: (0, 0)),
        pl.BlockSpec((None, None, C_WIDTH, N_MEM), lambda b, s: (layer, b, 0, 0)),
        pl.BlockSpec((None, None, N_MEM, C_WIDTH), lambda b, s: (layer, b, 0, 0)),
        lw(D_MODEL, IN_WIDTH), vec(IN_WIDTH),
        vec(A_WIDTH), vec(A_WIDTH),
        lw(A_GROUPS, CHUNK, CHUNK), lw(CHUNK, A_WIDTH),
        vec(CONV_WIDTH, B_WIDTH), vec(B_WIDTH),
        lw(B_HEADS, B_HEAD_DIM, 2 * B_HEAD_DIM), vec(B_WIDTH), vec(B_WIDTH), vec(B_WIDTH),
        lw(A_WIDTH, D_MODEL), lw(B_WIDTH, D_MODEL), lw(C_WIDTH, D_MODEL),
        lw(D_MODEL, D_MODEL), vec(D_MODEL), vec(D_MODEL), vec(D_MODEL),
    ]
    return pl.pallas_call(
        functools.partial(_mixer_kernel, layer),
        grid=(bsz, seq // tm),
        in_specs=in_specs,
        out_specs=pl.BlockSpec((None, tm, D_MODEL), lambda b, s: (b, s, 0)),
        out_shape=jax.ShapeDtypeStruct(x.shape, F32),
        scratch_shapes=[
            pltpu.VMEM((B_WIDTH // LANES, tm + SUBLANES * SUBLANES, LANES), F32),
            pltpu.VMEM(((CONV_WIDTH - 1) * SUBLANES, B_WIDTH), F32),
            pltpu.VMEM((1, B_WIDTH), F32),
        ],
        compiler_params=pltpu.CompilerParams(
            dimension_semantics=("arbitrary", "arbitrary"), vmem_limit_bytes=VMEM_LIMIT),
    )(x, lni_g, lni_b, kt, vm, w["w_in"], w["b_in"], w["ln_v_g"], w["ln_v_b"], w["w_s"], w["b_s"],
      w["conv_w"], w["conv_b"], w["w_ax"], w["b_a"], w["b_x"], w["lam"], w["p_a"], w["p_b"], w["p_c"],
      w["w_o"], w["b_o"], w["ln1_g"], w["ln1_b"])


def _first_max_of4(c):
    m = jnp.maximum(jnp.maximum(c[0], c[1]), jnp.maximum(c[2], c[3]))
    idx = jnp.where(c[0] == m, 0, jnp.where(c[1] == m, 1, jnp.where(c[2] == m, 2, 3)))
    return m, idx


def _moe_kernel(layer, x_next_ref, x_ref, wr_ref, br_ref, wup_src_ref, wdn_src_ref, ln2_g_ref, ln2_b_ref,
                out_ref, wup_scr, wdn_scr, g2_scr, pt_scr, dest_scr, ends_scr, f_scr):
    step = pl.program_id(0)

    @pl.when(step < N_EXPERTS)
    def _():
        wup_scr[step] = wup_src_ref[...].astype(BF16)
        grp = lax.div(step, EXPERTS_PER_GROUP)
        row0 = pl.multiple_of(lax.rem(step, EXPERTS_PER_GROUP) * D_EXPERT, D_EXPERT)
        wdn_scr[grp, pl.ds(row0, D_EXPERT), :] = wdn_src_ref[...].astype(BF16)

    pl.when(step >= N_EXPERTS)(functools.partial(
        _moe_tile, layer, step - N_EXPERTS, x_next_ref, x_ref, wr_ref, br_ref, wup_scr, wdn_scr,
        ln2_g_ref, ln2_b_ref, out_ref, g2_scr, pt_scr, dest_scr, ends_scr, f_scr))


def _moe_tile(layer, step, x_next_ref, x_ref, wr_ref, br_ref, wup_ref, wdn_ref, ln2_g_ref, ln2_b_ref,
              out_ref, g2_scr, pt_scr, dest_scr, ends_scr, f_scr):
    tm = x_ref.shape[0]
    new = lax.rem(step, 2)
    cur = 1 - new
    ln2_g_ref, ln2_b_ref = ln2_g_ref.at[layer:layer + 1], ln2_b_ref.at[layer:layer + 1]

    @pl.when(step == 0)
    def _():
        g2_scr[1] = jnp.zeros(g2_scr.shape[1:], BF16)
        pt_scr[1] = jnp.zeros(pt_scr.shape[1:], BF16)
        dest_scr[1] = jnp.zeros(dest_scr.shape[1:], jnp.int32)
        for g in range(N_GROUPS):
            ends_scr[1, g] = 0


    lgt = lax.dot_general(wr_ref[...], x_next_ref[...].astype(BF16), (((1,), (1,)), ((), ())),
                          preferred_element_type=F32)
    lt = lgt[:ROUTER_LANES, :] + lgt[ROUTER_LANES:, :] + br_ref[...]
    neg = -jnp.inf
    g_logit = [lt[g:g + 1, :] for g in range(N_GROUPS)]
    e_logit = [lt[N_GROUPS + e:N_GROUPS + e + 1, :] for e in range(N_EXPERTS)]

    gmax, g_top = _first_max_of4(g_logit)
    p_top = 1.0 / sum(jnp.exp(gl - gmax) for gl in g_logit)

    cand = [jnp.where(g_top == 0, e_logit[j], jnp.where(
        g_top == 1, e_logit[EXPERTS_PER_GROUP + j], jnp.where(
            g_top == 2, e_logit[2 * EXPERTS_PER_GROUP + j], e_logit[3 * EXPERTS_PER_GROUP + j])))
        for j in range(EXPERTS_PER_GROUP)]
    m1, i1 = _first_max_of4(cand)
    m2, i2 = _first_max_of4([jnp.where(i1 == j, neg, cand[j]) for j in range(EXPERTS_PER_GROUP)])
    e2 = jnp.exp(m2 - m1)
    w1 = p_top / (1.0 + e2)
    w2 = w1 * e2
    gates = [jnp.where(i1 == j, w1, 0.0) + jnp.where(i2 == j, w2, 0.0) for j in range(EXPERTS_PER_GROUP)]

    onehot8 = jnp.concatenate([jnp.where(g_top == g, 1.0, 0.0) for g in range(N_GROUPS)]
                              + [jnp.zeros((SUBLANES - N_GROUPS, tm), F32)], axis=0)
    upto = onehot8
    lane = lax.broadcasted_iota(jnp.int32, (SUBLANES, tm), 1)
    shift = 1
    while shift < tm:
        upto = upto + jnp.where(lane >= shift, pltpu.roll(upto, shift, 1), 0.0)
        shift *= 2
    rank = upto - onehot8
    dest_f = jnp.zeros((1, tm), F32)
    new_ends = []
    end = 0
    for g in range(N_GROUPS):
        count = upto[g:g + 1, tm - 1:tm][0, 0].astype(jnp.int32)
        base = end.astype(F32) if g else 0.0
        dest_f = dest_f + onehot8[g:g + 1, :] * (rank[g:g + 1, :] + base)
        n_blocks = lax.shift_right_logical(count + (ROW_BLOCK - 1), ROW_BLOCK.bit_length() - 1)
        end = end + n_blocks * ROW_BLOCK
        new_ends.append(end)
    per_token = jnp.concatenate(
        [dest_f] + gates + [jnp.zeros((LANES - 1 - EXPERTS_PER_GROUP, tm), F32)], axis=0).T

    ends = [ends_scr[cur, g] for g in range(N_GROUPS)]
    dest_row = dest_scr[cur, 0:1, :]
    block_row = lax.broadcasted_iota(jnp.int32, (ROW_BLOCK, tm), 0)

    def expert_block(b):
        start = b * ROW_BLOCK
        grp = sum((start >= ends[g]).astype(jnp.int32) for g in range(N_GROUPS - 1))
        perm = jnp.where(block_row + start == dest_row, 1.0, 0.0).astype(BF16)
        xsb = _dot(perm, x_ref[...].astype(BF16)).astype(BF16)
        gs2 = _dot(perm, g2_scr[cur])
        gsb = gs2[:, :ROUTER_LANES] + gs2[:, ROUTER_LANES:]
        acts = []
        for j in range(EXPERTS_PER_GROUP):
            h = _dot(xsb, wup_ref[grp * EXPERTS_PER_GROUP + j])
            hg = h[:, :D_EXPERT]
            acts.append((_silu(hg) * h[:, D_EXPERT:] * gsb[:, j + 1:j + 2]).astype(BF16))
        f_scr[start:start + ROW_BLOCK, :] = _dot(jnp.concatenate(acts, axis=1), wdn_ref[grp]).astype(BF16)

    for b in range(ALWAYS_BLOCKS):
        expert_block(b)

    for g in range(N_GROUPS):
        ends_scr[new, g] = new_ends[g]
    dest_scr[new] = jnp.broadcast_to(dest_f.astype(jnp.int32), (SUBLANES, tm))
    pt_scr[new] = jnp.where(lax.broadcasted_iota(jnp.int32, (tm, SORTED_ROWS), 1)
                            == per_token[:, 0:1].astype(jnp.int32), 1.0, 0.0).astype(BF16)
    g_hi = per_token.astype(BF16)
    g2_scr[new] = jnp.concatenate([g_hi, (per_token - g_hi.astype(F32)).astype(BF16)], axis=1)

    for b in range(ALWAYS_BLOCKS, MAX_BLOCKS):
        used = b * ROW_BLOCK < ends[N_GROUPS - 1]
        pl.when(used)(functools.partial(expert_block, b))

        @pl.when(jnp.logical_not(used))
        def _():
            f_scr[b * ROW_BLOCK:(b + 1) * ROW_BLOCK, :] = jnp.zeros((ROW_BLOCK, D_MODEL), BF16)

    def finish(n_rows):
        for r0 in range(0, tm, FINISH_ROWS):
            rows = slice(r0, r0 + FINISH_ROWS)
            f = _dot(pt_scr[cur, rows, :n_rows], f_scr[:n_rows, :])
            out_ref[rows, :] = _layer_norm(ALPHA * x_ref[rows, :] + f, ln2_g_ref[...], ln2_b_ref[...])

    finish(SORTED_ROWS - ROW_BLOCK)
    pl.when(SORTED_ROWS - ROW_BLOCK < ends[N_GROUPS - 1])(functools.partial(finish, SORTED_ROWS))


def _moe_call(layer, x2d, w):
    n_tok = x2d.shape[0]
    tm = MOE_TILE

    def lw(*shape):
        return _resident((None,) + shape, lambda t: (layer,) + (0,) * len(shape))

    n_tiles = n_tok // tm

    def routed(s):
        return jnp.clip(s - N_EXPERTS, 0, n_tiles - 1)

    def finished(s):
        return jnp.clip(s - N_EXPERTS - 1, 0, n_tiles - 1)

    def expert(s):
        return jnp.minimum(s, N_EXPERTS - 1)

    return pl.pallas_call(
        functools.partial(_moe_kernel, layer),
        grid=(N_EXPERTS + n_tiles + 1,),
        in_specs=[
            pl.BlockSpec((tm, D_MODEL), lambda s: (routed(s), 0)),
            pl.BlockSpec((tm, D_MODEL), lambda s: (finished(s), 0)),
            lw(2 * ROUTER_LANES, D_MODEL), lw(ROUTER_LANES, 1),
            pl.BlockSpec((None, None, D_MODEL, 2 * D_EXPERT), lambda s: (layer, expert(s), 0, 0)),
            pl.BlockSpec((None, None, D_EXPERT, D_MODEL), lambda s: (layer, expert(s), 0, 0)),
            _resident((DEPTH, D_MODEL), lambda s: (0, 0)), _resident((DEPTH, D_MODEL), lambda s: (0, 0)),
        ],
        out_specs=pl.BlockSpec((tm, D_MODEL), lambda s: (finished(s), 0)),
        out_shape=jax.ShapeDtypeStruct(x2d.shape, F32),
        scratch_shapes=[
            pltpu.VMEM((N_EXPERTS, D_MODEL, 2 * D_EXPERT), BF16),
            pltpu.VMEM((N_GROUPS, EXPERTS_PER_GROUP * D_EXPERT, D_MODEL), BF16),
            pltpu.VMEM((2, tm, 2 * ROUTER_LANES), BF16),
            pltpu.VMEM((2, tm, SORTED_ROWS), BF16),
            pltpu.VMEM((2, SUBLANES, tm), jnp.int32),
            pltpu.SMEM((2, N_GROUPS), jnp.int32),
            pltpu.VMEM((SORTED_ROWS, D_MODEL), BF16),
        ],
        compiler_params=pltpu.CompilerParams(
            dimension_semantics=("arbitrary",), vmem_limit_bytes=VMEM_LIMIT),
    )(x2d, x2d, w["wr"], w["br"], w["w_up"], w["w_down"], w["ln2_g"], w["ln2_b"])


def kernel(x, mem, ln_in_g, ln_in_b, ln_mem_g, ln_mem_b, w_in, b_in, ln_v_g, ln_v_b, w_s, b_s, conv_w, conv_b, w_a, b_a, w_x, b_x, lam, w_kv, p_a, p_b, p_c, w_o, b_o, ln1_g, ln1_b, w_rg, b_rg, w_re, b_re, w_up, w_down, ln2_g, ln2_b):
    bsz, seq, _ = x.shape
    assert seq % TOKEN_TILE == 0 and TOKEN_TILE % CHUNK == 0 and (bsz * seq) % MOE_TILE == 0

    wr = jnp.concatenate(
        [w_rg, w_re, jnp.zeros((DEPTH, D_MODEL, ROUTER_LANES - N_GROUPS - N_EXPERTS), F32)], axis=-1)
    wr_hi = wr.astype(BF16)
    in_scale = jnp.where(jnp.arange(IN_WIDTH) >= OFF_M, 0.5, 1.0).astype(F32)
    w = {
        "w_in": (w_in * in_scale).astype(BF16), "b_in": b_in,
        "ln_v_g": ln_v_g, "ln_v_b": ln_v_b,
        "w_s": w_s.astype(BF16),
        "b_s": jnp.repeat(jnp.swapaxes(b_s, 1, 2), A_WIDTH // A_GROUPS, axis=-1),
        "conv_w": conv_w, "conv_b": conv_b,
        "w_ax": (0.5 * jnp.concatenate([w_a, w_x], axis=-1)).astype(BF16),
        "b_a": b_a, "b_x": b_x, "lam": lam,
        "p_a": p_a.astype(BF16), "p_b": p_b.astype(BF16), "p_c": p_c.astype(BF16),
        "w_o": w_o.astype(BF16), "b_o": b_o, "ln1_g": ln1_g, "ln1_b": ln1_b,
        "wr": jnp.swapaxes(jnp.concatenate([wr_hi, (wr - wr_hi.astype(F32)).astype(BF16)], axis=-1), 1, 2),
        "br": jnp.concatenate(
            [b_rg, b_re, jnp.zeros((DEPTH, ROUTER_LANES - N_GROUPS - N_EXPERTS), F32)], axis=-1)[:, :, None],
        "w_up": w_up, "w_down": w_down,
        "ln2_g": ln2_g, "ln2_b": ln2_b,
    }
    lni_g = ln_in_g.reshape(1, D_MODEL)
    lni_b = ln_in_b.reshape(1, D_MODEL)

    kt, vm = _prep_call(mem, ln_mem_g, ln_mem_b, w_kv.astype(BF16))
    for layer in range(DEPTH):
        x = _mixer_call(layer, x, lni_g, lni_b, kt, vm, w)
        x = _moe_call(layer, x.reshape(bsz * seq, D_MODEL), w).reshape(bsz, seq, D_MODEL)
    return x
```

```python
import functools
import math

import jax
import jax.numpy as jnp
from jax import lax
from jax.experimental import pallas as pl
from jax.experimental.pallas import tpu as pltpu

D_MODEL = 1024
DEPTH = 2
N_MEM = 256
CHUNK = 128
A_GROUPS = 8
A_WIDTH = 1024
B_HEADS = 10
B_WIDTH = 1280
B_HEAD_DIM = B_WIDTH // B_HEADS
CONV_WIDTH = 4
LRU_C = 8.0
C_HEADS = 4
C_HEAD_DIM = 256
C_WIDTH = C_HEADS * C_HEAD_DIM
IN_WIDTH = 2 * A_WIDTH + 2 * B_WIDTH + C_WIDTH + 3 * D_MODEL
OFF_U = 0
OFF_V = A_WIDTH
OFF_XB = 2 * A_WIDTH
OFF_GB = OFF_XB + B_WIDTH
OFF_Q = OFF_GB + B_WIDTH
OFF_M = OFF_Q + C_WIDTH
N_GROUPS = 4
EXPERTS_PER_GROUP = 4
N_EXPERTS = 16
D_EXPERT = 256
ALPHA = (2 * DEPTH) ** 0.25
LN_EPS = 1e-5
LOG2_E = 1.0 / math.log(2.0)
GELU_K1 = -2.0 * math.sqrt(2.0 / math.pi) * LOG2_E
GELU_K3 = 0.044715 * GELU_K1

SUBLANES = 8
LANES = 128
ROUTER_LANES = LANES
TOKEN_TILE = 512
MOE_TILE = 512
ROW_BLOCK = 128
MAX_BLOCKS = MOE_TILE // ROW_BLOCK + N_GROUPS
ALWAYS_BLOCKS = MOE_TILE // ROW_BLOCK + 1
SORTED_ROWS = MAX_BLOCKS * ROW_BLOCK
FINISH_ROWS = 128
VMEM_LIMIT = 56 * 1024 * 1024

BF16 = jnp.bfloat16
F32 = jnp.float32


def _dot(a, b):
    return jnp.dot(a, b, preferred_element_type=F32)


def _layer_norm(x, g, b):
    mu = jnp.mean(x, axis=-1, keepdims=True)
    xc = x - mu
    var = jnp.mean(xc * xc, axis=-1, keepdims=True)
    return xc * lax.rsqrt(var + LN_EPS) * g + b


def _gelu(x):
    return x * (1.0 / (1.0 + jnp.exp2(x * (GELU_K1 + GELU_K3 * (x * x)))))


def _sigmoid_of_twice(h):
    return 0.5 + 0.5 * jnp.tanh(h)


def _silu(x):
    return x * (1.0 / (1.0 + jnp.exp2(-LOG2_E * x)))


def _sqrt_nonneg(y):
    return jnp.exp2(jnp.log(y) * (0.5 * LOG2_E))


def _prep_kernel(mem_ref, g_ref, b_ref, wkv_ref, kt_ref, v_ref):
    bsz = mem_ref.shape[0]
    mem = mem_ref[...].reshape(bsz * N_MEM, D_MODEL)
    mem_n = _layer_norm(mem, g_ref[...], b_ref[...]).astype(BF16)
    kv = _dot(mem_n, wkv_ref[...])
    for b in range(bsz):
        rows = slice(b * N_MEM, (b + 1) * N_MEM)
        kt_ref[b] = kv[rows, :C_WIDTH].T.astype(BF16)
        v_ref[b] = kv[rows, C_WIDTH:].astype(BF16)


def _prep_call(mem, ln_g, ln_b, w_kv_bf):
    bsz = mem.shape[0]
    return pl.pallas_call(
        _prep_kernel,
        grid=(DEPTH,),
        in_specs=[
            pl.BlockSpec((bsz, N_MEM, D_MODEL), lambda l: (0, 0, 0)),
            pl.BlockSpec((1, D_MODEL), lambda l: (0, 0)),
            pl.BlockSpec((1, D_MODEL), lambda l: (0, 0)),
            pl.BlockSpec((None, D_MODEL, 2 * C_WIDTH), lambda l: (l, 0, 0)),
        ],
        out_specs=[
            pl.BlockSpec((None, bsz, C_WIDTH, N_MEM), lambda l: (l, 0, 0, 0)),
            pl.BlockSpec((None, bsz, N_MEM, C_WIDTH), lambda l: (l, 0, 0, 0)),
        ],
        out_shape=[
            jax.ShapeDtypeStruct((DEPTH, bsz, C_WIDTH, N_MEM), BF16),
            jax.ShapeDtypeStruct((DEPTH, bsz, N_MEM, C_WIDTH), BF16),
        ],
        compiler_params=pltpu.CompilerParams(
            dimension_semantics=("arbitrary",), vmem_limit_bytes=VMEM_LIMIT),
    )(mem, ln_g.reshape(1, D_MODEL), ln_b.reshape(1, D_MODEL), w_kv_bf)


def _mixer_kernel(layer, x_ref, lni_g_ref, lni_b_ref, kt_ref, vm_ref, w_in_ref, b_in_ref,
                  lnv_g_ref, lnv_b_ref, ws_ref, bs_ref, convw_ref, convb_ref, wax_ref, ba_ref, bx_ref,
                  lam_ref, pa_ref, pb_ref, pc_ref, wo_ref, bo_ref, ln1_g_ref, ln1_b_ref,
                  out_ref, seg_scr, tail_scr, hlast_scr):
    tm = x_ref.shape[0]
    pre_ln = layer == 0
    (b_in_ref, lnv_g_ref, lnv_b_ref, convb_ref, ba_ref, bx_ref, lam_ref, bo_ref, ln1_g_ref, ln1_b_ref) = (
        r.at[layer:layer + 1] for r in (b_in_ref, lnv_g_ref, lnv_b_ref, convb_ref, ba_ref, bx_ref, lam_ref,
                                        bo_ref, ln1_g_ref, ln1_b_ref))
    convw_ref = convw_ref.at[layer]

    @pl.when(pl.program_id(1) == 0)
    def _():
        tail_scr[...] = jnp.zeros(tail_scr.shape, F32)
        hlast_scr[...] = jnp.zeros((1, B_WIDTH), F32)

    x = x_ref[...]
    if pre_ln:
        x = _layer_norm(x, lni_g_ref[...], lni_b_ref[...])
    xb = x.astype(BF16)

    def proj(lo, width):
        bias = b_in_ref[:, lo:lo + width]
        if lo >= OFF_M:
            bias = 0.5 * bias
        return _dot(xb, w_in_ref[:, lo:lo + width]) + bias

    v_pre = proj(OFF_V, A_WIDTH)
    xbr = proj(OFF_XB, B_WIDTH)
    vn = _layer_norm(_gelu(v_pre), lnv_g_ref[...], lnv_b_ref[...]).astype(BF16)
    u_pre = proj(OFF_U, A_WIDTH)

    seg = tm // SUBLANES
    pitch = seg + SUBLANES
    n_slab = B_WIDTH // LANES
    sub = lax.broadcasted_iota(jnp.int32, (SUBLANES, B_WIDTH), 0)
    for s in range(SUBLANES):
        for l in range(n_slab):
            seg_scr[l, pitch * s:pitch * s + seg, :] = xbr[seg * s:seg * (s + 1), LANES * l:LANES * (l + 1)]
    xs = [jnp.concatenate([seg_scr[l, pl.ds(j, SUBLANES, stride=pitch), :] for l in range(n_slab)], axis=1)
          for j in range(seg)]

    def wrap(cur, prv):
        return pltpu.roll(jnp.where(sub == SUBLANES - 1, prv, cur), 1, 0)

    n_tail = CONV_WIDTH - 1
    ext = [wrap(xs[seg - n_tail + k], tail_scr[SUBLANES * k:SUBLANES * (k + 1), :]) for k in range(n_tail)] + xs
    for k in range(n_tail):
        tail_scr[SUBLANES * k:SUBLANES * (k + 1), :] = xs[seg - n_tail + k]
    conv_w = [jnp.broadcast_to(convw_ref[k:k + 1, :], (SUBLANES, B_WIDTH)) for k in range(CONV_WIDTH)]
    conv_b = jnp.broadcast_to(convb_ref[...], (SUBLANES, B_WIDTH))
    conv_steps = []
    for j in range(seg):
        acc = conv_b
        for k in range(CONV_WIDTH):
            acc = acc + conv_w[k] * ext[j + k]
        conv_steps.append(acc)
    conv = jnp.concatenate(conv_steps, axis=0)
    cb = conv.astype(BF16)
    gb_pre = proj(OFF_GB, B_WIDTH)
    u = _gelu(u_pre)

    row = lax.broadcasted_iota(jnp.int32, (CHUNK, CHUNK), 0)
    col = lax.broadcasted_iota(jnp.int32, (CHUNK, CHUNK), 1)
    causal = col <= row
    ws = [jnp.where(causal, ws_ref[g], jnp.zeros((CHUNK, CHUNK), BF16)) for g in range(A_GROUPS)]
    s_chunks = []
    for c in range(tm // CHUNK):
        vc = vn[c * CHUNK:(c + 1) * CHUNK, :]
        gd = A_WIDTH // A_GROUPS
        s_c = jnp.concatenate(
            [_dot(ws[g], vc[:, g * gd:(g + 1) * gd]) for g in range(A_GROUPS)], axis=1)
        s_chunks.append(s_c + bs_ref[...])
    gate_b = _gelu(gb_pre)
    q = proj(OFF_Q, C_WIDTH).astype(BF16)
    o_a = (u * jnp.concatenate(s_chunks, axis=0)).astype(BF16)
    ri = jnp.concatenate(
        [_dot(cb[:, h * B_HEAD_DIM:(h + 1) * B_HEAD_DIM], wax_ref[h]) for h in range(B_HEADS)],
        axis=1)
    m_a = proj(OFF_M, D_MODEL)

    r_parts = [ri[:, (2 * h) * B_HEAD_DIM:(2 * h + 1) * B_HEAD_DIM] for h in range(B_HEADS)]
    i_parts = [ri[:, (2 * h + 1) * B_HEAD_DIM:(2 * h + 2) * B_HEAD_DIM] for h in range(B_HEADS)]
    r = _sigmoid_of_twice(jnp.concatenate(r_parts, axis=1) + 0.5 * ba_ref[...])
    i = _sigmoid_of_twice(jnp.concatenate(i_parts, axis=1) + 0.5 * bx_ref[...])
    z = -lam_ref[...]
    softplus = jnp.maximum(z, 0.0) + jnp.log(1.0 + jnp.exp(-jnp.abs(z)))
    log_a = (-LRU_C * softplus) * r
    a = jnp.exp(log_a)
    bx = _sqrt_nonneg(1.0 - a * a) * (i * conv)
    a_steps = [a[SUBLANES * j:SUBLANES * (j + 1), :] for j in range(seg)]
    b_steps = [bx[SUBLANES * j:SUBLANES * (j + 1), :] for j in range(seg)]
    y_a = _dot(o_a, pa_ref[...])
    scores = [_dot(q[:, h * C_HEAD_DIM:(h + 1) * C_HEAD_DIM], kt_ref[h * C_HEAD_DIM:(h + 1) * C_HEAD_DIM, :])
              for h in range(C_HEADS)]

    h_loc, p_tot = b_steps[0], a_steps[0]
    for j in range(1, seg):
        h_loc = a_steps[j] * h_loc + b_steps[j]
        p_tot = p_tot * a_steps[j]
    for d in (1, 2, 4):
        m = sub >= d
        h_sh = jnp.where(m, pltpu.roll(h_loc, d, 0), 0.0)
        p_sh = jnp.where(m, pltpu.roll(p_tot, d, 0), 1.0)
        h_loc = p_tot * h_sh + h_loc
        p_tot = p_tot * p_sh
    h_last = hlast_scr[...]
    h_end = h_loc + p_tot * h_last
    hlast_scr[...] = h_end[SUBLANES - 1:SUBLANES, :]
    h = jnp.where(sub == 0, h_last, pltpu.roll(h_end, 1, 0))
    for j in range(seg):
        h = a_steps[j] * h + b_steps[j]
        for l in range(n_slab):
            seg_scr[l, pl.ds(j, SUBLANES, stride=pitch), :] = h[:, LANES * l:LANES * (l + 1)]
    h_tok = jnp.concatenate(
        [jnp.concatenate([seg_scr[l, pitch * s:pitch * s + seg, :] for l in range(n_slab)], axis=1)
         for s in range(SUBLANES)], axis=0)
    o_b = (h_tok * gate_b).astype(BF16)

    m_b = proj(OFF_M + D_MODEL, D_MODEL)
    probs, denoms = [], []
    for h in range(C_HEADS):
        s = scores[h] * (C_HEAD_DIM ** -0.5)
        p = jnp.exp(s - jnp.max(s, axis=-1, keepdims=True))
        denoms.append(jnp.sum(p, axis=-1, keepdims=True))
        probs.append(p.astype(BF16))
    m_c = proj(OFF_M + 2 * D_MODEL, D_MODEL)
    y_b = _dot(o_b, pb_ref[...])
    o_c = jnp.concatenate(
        [_dot(probs[h], vm_ref[:, h * C_HEAD_DIM:(h + 1) * C_HEAD_DIM]) / denoms[h] for h in range(C_HEADS)],
        axis=1).astype(BF16)
    y_ab = _sigmoid_of_twice(m_a) * y_a + _sigmoid_of_twice(m_b) * y_b
    y_c = _dot(o_c, pc_ref[...])

    y = y_ab + _sigmoid_of_twice(m_c) * y_c
    m = _dot(y.astype(BF16), wo_ref[...]) + bo_ref[...]
    out_ref[...] = _layer_norm(ALPHA * x + m, ln1_g_ref[...], ln1_b_ref[...])


def _resident(block_shape, index_map):
    return pl.BlockSpec(block_shape, index_map, pipeline_mode=pl.Buffered(1))


def _mixer_call(layer, x, lni_g, lni_b, kt, vm, w):
    bsz, seq, _ = x.shape
    tm = TOKEN_TILE

    def lw(*shape):
        return _resident((None,) + shape, lambda b, s: (layer,) + (0,) * len(shape))

    def vec(*shape):
        return _resident((DEPTH,) + shape, lambda b, s: (0,) * (1 + len(shape)))

    in_specs = [
        pl.BlockSpec((None, tm, D_MODEL), lambda b, s: (b, s, 0)),
        _resident((1, D_MODEL), lambda b, s: (0, 0)),
        _resident((1, D_MODEL), lambda b, s: (0, 0)),
        pl.BlockSpec((None, None, C_WIDTH, N_MEM), lambda b, s: (layer, b, 0, 0)),
        pl.BlockSpec((None, None, N_MEM, C_WIDTH), lambda b, s: (layer, b, 0, 0)),
        lw(D_MODEL, IN_WIDTH), vec(IN_WIDTH),
        vec(A_WIDTH), vec(A_WIDTH),
        lw(A_GROUPS, CHUNK, CHUNK), lw(CHUNK, A_WIDTH),
        vec(CONV_WIDTH, B_WIDTH), vec(B_WIDTH),
        lw(B_HEADS, B_HEAD_DIM, 2 * B_HEAD_DIM), vec(B_WIDTH), vec(B_WIDTH), vec(B_WIDTH),
        lw(A_WIDTH, D_MODEL), lw(B_WIDTH, D_MODEL), lw(C_WIDTH, D_MODEL),
        lw(D_MODEL, D_MODEL), vec(D_MODEL), vec(D_MODEL), vec(D_MODEL),
    ]
    return pl.pallas_call(
        functools.partial(_mixer_kernel, layer),
        grid=(bsz, seq // tm),
        in_specs=in_specs,
        out_specs=pl.BlockSpec((None, tm, D_MODEL), lambda b, s: (b, s, 0)),
        out_shape=jax.ShapeDtypeStruct(x.shape, F32),
        scratch_shapes=[
            pltpu.VMEM((B_WIDTH // LANES, tm + SUBLANES * SUBLANES, LANES), F32),
            pltpu.VMEM(((CONV_WIDTH - 1) * SUBLANES, B_WIDTH), F32),
            pltpu.VMEM((1, B_WIDTH), F32),
        ],
        compiler_params=pltpu.CompilerParams(
            dimension_semantics=("arbitrary", "arbitrary"), vmem_limit_bytes=VMEM_LIMIT),
    )(x, lni_g, lni_b, kt, vm, w["w_in"], w["b_in"], w["ln_v_g"], w["ln_v_b"], w["w_s"], w["b_s"],
      w["conv_w"], w["conv_b"], w["w_ax"], w["b_a"], w["b_x"], w["lam"], w["p_a"], w["p_b"], w["p_c"],
      w["w_o"], w["b_o"], w["ln1_g"], w["ln1_b"])


def _first_max_of4(c):
    m = jnp.maximum(jnp.maximum(c[0], c[1]), jnp.maximum(c[2], c[3]))
    idx = jnp.where(c[0] == m, 0, jnp.where(c[1] == m, 1, jnp.where(c[2] == m, 2, 3)))
    return m, idx


def _moe_kernel(layer, x_next_ref, x_ref, wr_ref, br_ref, wup_src_ref, wdn_src_ref, ln2_g_ref, ln2_b_ref,
                out_ref, wup_scr, wdn_scr, g2_scr, pt_scr, dest_scr, ends_scr, f_scr):
    step = pl.program_id(0)

    @pl.when(step < N_EXPERTS)
    def _():
        wup_scr[step] = wup_src_ref[...].astype(BF16)
        grp = lax.div(step, EXPERTS_PER_GROUP)
        row0 = pl.multiple_of(lax.rem(step, EXPERTS_PER_GROUP) * D_EXPERT, D_EXPERT)
        wdn_scr[grp, pl.ds(row0, D_EXPERT), :] = wdn_src_ref[...].astype(BF16)

    @pl.when(step == N_EXPERTS - 1)
    def _():
        _store_route(0, *_route(x_next_ref, wr_ref, br_ref), g2_scr, pt_scr, dest_scr, ends_scr)

    pl.when(step >= N_EXPERTS)(functools.partial(
        _moe_tile, layer, step - (N_EXPERTS - 1), x_next_ref, x_ref, wr_ref, br_ref, wup_scr, wdn_scr,
        ln2_g_ref, ln2_b_ref, out_ref, g2_scr, pt_scr, dest_scr, ends_scr, f_scr))


def _moe_tile(layer, step, x_next_ref, x_ref, wr_ref, br_ref, wup_ref, wdn_ref, ln2_g_ref, ln2_b_ref,
              out_ref, g2_scr, pt_scr, dest_scr, ends_scr, f_scr):
    tm = x_ref.shape[0]
    new = lax.rem(step, 2)
    cur = 1 - new
    ln2_g_ref, ln2_b_ref = ln2_g_ref.at[layer:layer + 1], ln2_b_ref.at[layer:layer + 1]

    new_ends, dest_f, per_token = _route(x_next_ref, wr_ref, br_ref)
    _moe_experts_and_store(new, cur, new_ends, dest_f, per_token, x_ref, wup_ref, wdn_ref, ln2_g_ref, ln2_b_ref,
                           out_ref, g2_scr, pt_scr, dest_scr, ends_scr, f_scr)


def _route(x_next_ref, wr_ref, br_ref):
    tm = x_next_ref.shape[0]
    lgt = lax.dot_general(wr_ref[...], x_next_ref[...].astype(BF16), (((1,), (1,)), ((), ())),
                          preferred_element_type=F32)
    lt = lgt[:ROUTER_LANES, :] + lgt[ROUTER_LANES:, :] + br_ref[...]
    neg = -jnp.inf
    g_logit = [lt[g:g + 1, :] for g in range(N_GROUPS)]
    e_logit = [lt[N_GROUPS + e:N_GROUPS + e + 1, :] for e in range(N_EXPERTS)]

    gmax, g_top = _first_max_of4(g_logit)
    p_top = 1.0 / sum(jnp.exp(gl - gmax) for gl in g_logit)

    cand = [jnp.where(g_top == 0, e_logit[j], jnp.where(
        g_top == 1, e_logit[EXPERTS_PER_GROUP + j], jnp.where(
            g_top == 2, e_logit[2 * EXPERTS_PER_GROUP + j], e_logit[3 * EXPERTS_PER_GROUP + j])))
        for j in range(EXPERTS_PER_GROUP)]
    m1, i1 = _first_max_of4(cand)
    m2, i2 = _first_max_of4([jnp.where(i1 == j, neg, cand[j]) for j in range(EXPERTS_PER_GROUP)])
    e2 = jnp.exp(m2 - m1)
    w1 = p_top / (1.0 + e2)
    w2 = w1 * e2
    gates = [jnp.where(i1 == j, w1, 0.0) + jnp.where(i2 == j, w2, 0.0) for j in range(EXPERTS_PER_GROUP)]

    onehot8 = jnp.concatenate([jnp.where(g_top == g, 1.0, 0.0) for g in range(N_GROUPS)]
                              + [jnp.zeros((SUBLANES - N_GROUPS, tm), F32)], axis=0)
    upto = onehot8
    lane = lax.broadcasted_iota(jnp.int32, (SUBLANES, tm), 1)
    shift = 1
    while shift < tm:
        upto = upto + jnp.where(lane >= shift, pltpu.roll(upto, shift, 1), 0.0)
        shift *= 2
    rank = upto - onehot8
    dest_f = jnp.zeros((1, tm), F32)
    new_ends = []
    end = 0
    for g in range(N_GROUPS):
        count = upto[g:g + 1, tm - 1:tm][0, 0].astype(jnp.int32)
        base = end.astype(F32) if g else 0.0
        dest_f = dest_f + onehot8[g:g + 1, :] * (rank[g:g + 1, :] + base)
        n_blocks = lax.shift_right_logical(count + (ROW_BLOCK - 1), ROW_BLOCK.bit_length() - 1)
        end = end + n_blocks * ROW_BLOCK
        new_ends.append(end)
    per_token = jnp.concatenate(
        [dest_f] + gates + [jnp.zeros((LANES - 1 - EXPERTS_PER_GROUP, tm), F32)], axis=0).T
    return new_ends, dest_f, per_token


def _store_route(slot, new_ends, dest_f, per_token, g2_scr, pt_scr, dest_scr, ends_scr):
    tm = per_token.shape[0]
    for g in range(N_GROUPS):
        ends_scr[slot, g] = new_ends[g]
    dest_scr[slot] = jnp.broadcast_to(dest_f.astype(jnp.int32), (SUBLANES, tm))
    pt_scr[slot] = jnp.where(lax.broadcasted_iota(jnp.int32, (tm, SORTED_ROWS), 1)
                             == per_token[:, 0:1].astype(jnp.int32), 1.0, 0.0).astype(BF16)
    g_hi = per_token.astype(BF16)
    g2_scr[slot] = jnp.concatenate([g_hi, (per_token - g_hi.astype(F32)).astype(BF16)], axis=1)


def _moe_experts_and_store(new, cur, new_ends, dest_f, per_token, x_ref, wup_ref, wdn_ref, ln2_g_ref, ln2_b_ref,
                           out_ref, g2_scr, pt_scr, dest_scr, ends_scr, f_scr):
    tm = x_ref.shape[0]
    ends = [ends_scr[cur, g] for g in range(N_GROUPS)]
    dest_row = dest_scr[cur, 0:1, :]
    block_row = lax.broadcasted_iota(jnp.int32, (ROW_BLOCK, tm), 0)

    def expert_block(b):
        start = b * ROW_BLOCK
        grp = sum((start >= ends[g]).astype(jnp.int32) for g in range(N_GROUPS - 1))
        perm = jnp.where(block_row + start == dest_row, 1.0, 0.0).astype(BF16)
        xsb = _dot(perm, x_ref[...].astype(BF16)).astype(BF16)
        gs2 = _dot(perm, g2_scr[cur])
        gsb = gs2[:, :ROUTER_LANES] + gs2[:, ROUTER_LANES:]
        acts = []
        for j in range(EXPERTS_PER_GROUP):
            h = _dot(xsb, wup_ref[grp * EXPERTS_PER_GROUP + j])
            hg = h[:, :D_EXPERT]
            acts.append((_silu(hg) * h[:, D_EXPERT:] * gsb[:, j + 1:j + 2]).astype(BF16))
        f_scr[start:start + ROW_BLOCK, :] = _dot(jnp.concatenate(acts, axis=1), wdn_ref[grp]).astype(BF16)

    for b in range(ALWAYS_BLOCKS):
        expert_block(b)

    _store_route(new, new_ends, dest_f, per_token, g2_scr, pt_scr, dest_scr, ends_scr)

    for b in range(ALWAYS_BLOCKS, MAX_BLOCKS):
        used = b * ROW_BLOCK < ends[N_GROUPS - 1]
        pl.when(used)(functools.partial(expert_block, b))

        @pl.when(jnp.logical_not(used))
        def _():
            f_scr[b * ROW_BLOCK:(b + 1) * ROW_BLOCK, :] = jnp.zeros((ROW_BLOCK, D_MODEL), BF16)

    def finish(n_rows):
        for r0 in range(0, tm, FINISH_ROWS):
            rows = slice(r0, r0 + FINISH_ROWS)
            f = _dot(pt_scr[cur, rows, :n_rows], f_scr[:n_rows, :])
            out_ref[rows, :] = _layer_norm(ALPHA * x_ref[rows, :] + f, ln2_g_ref[...], ln2_b_ref[...])

    finish(SORTED_ROWS - ROW_BLOCK)
    pl.when(SORTED_ROWS - ROW_BLOCK < ends[N_GROUPS - 1])(functools.partial(finish, SORTED_ROWS))


def _moe_call(layer, x2d, w):
    n_tok = x2d.shape[0]
    tm = MOE_TILE

    def lw(*shape):
        return _resident((None,) + shape, lambda t: (layer,) + (0,) * len(shape))

    n_tiles = n_tok // tm

    def routed(s):
        return jnp.clip(s - (N_EXPERTS - 1), 0, n_tiles - 1)

    def finished(s):
        return jnp.clip(s - N_EXPERTS, 0, n_tiles - 1)

    def expert(s):
        return jnp.minimum(s, N_EXPERTS - 1)

    return pl.pallas_call(
        functools.partial(_moe_kernel, layer),
        grid=(N_EXPERTS + n_tiles,),
        in_specs=[
            pl.BlockSpec((tm, D_MODEL), lambda s: (routed(s), 0)),
            pl.BlockSpec((tm, D_MODEL), lambda s: (finished(s), 0)),
            lw(2 * ROUTER_LANES, D_MODEL), lw(ROUTER_LANES, 1),
            pl.BlockSpec((None, None, D_MODEL, 2 * D_EXPERT), lambda s: (layer, expert(s), 0, 0)),
            pl.BlockSpec((None, None, D_EXPERT, D_MODEL), lambda s: (layer, expert(s), 0, 0)),
            _resident((DEPTH, D_MODEL), lambda s: (0, 0)), _resident((DEPTH, D_MODEL), lambda s: (0, 0)),
        ],
        out_specs=pl.BlockSpec((tm, D_MODEL), lambda s: (finished(s), 0)),
        out_shape=jax.ShapeDtypeStruct(x2d.shape, F32),
        scratch_shapes=[
            pltpu.VMEM((N_EXPERTS, D_MODEL, 2 * D_EXPERT), BF16),
            pltpu.VMEM((N_GROUPS, EXPERTS_PER_GROUP * D_EXPERT, D_MODEL), BF16),
            pltpu.VMEM((2, tm, 2 * ROUTER_LANES), BF16),
            pltpu.VMEM((2, tm, SORTED_ROWS), BF16),
            pltpu.VMEM((2, SUBLANES, tm), jnp.int32),
            pltpu.SMEM((2, N_GROUPS), jnp.int32),
            pltpu.VMEM((SORTED_ROWS, D_MODEL), BF16),
        ],
        compiler_params=pltpu.CompilerParams(
            dimension_semantics=("arbitrary",), vmem_limit_bytes=VMEM_LIMIT),
    )(x2d, x2d, w["wr"], w["br"], w["w_up"], w["w_down"], w["ln2_g"], w["ln2_b"])


def kernel(x, mem, ln_in_g, ln_in_b, ln_mem_g, ln_mem_b, w_in, b_in, ln_v_g, ln_v_b, w_s, b_s, conv_w, conv_b, w_a, b_a, w_x, b_x, lam, w_kv, p_a, p_b, p_c, w_o, b_o, ln1_g, ln1_b, w_rg, b_rg, w_re, b_re, w_up, w_down, ln2_g, ln2_b):
    bsz, seq, _ = x.shape
    assert seq % TOKEN_TILE == 0 and TOKEN_TILE % CHUNK == 0 and (bsz * seq) % MOE_TILE == 0

    wr = jnp.concatenate(
        [w_rg, w_re, jnp.zeros((DEPTH, D_MODEL, ROUTER_LANES - N_GROUPS - N_EXPERTS), F32)], axis=-1)
    wr_hi = wr.astype(BF16)
    in_scale = jnp.where(jnp.arange(IN_WIDTH) >= OFF_M, 0.5, 1.0).astype(F32)
    w = {
        "w_in": (w_in * in_scale).astype(BF16), "b_in": b_in,
        "ln_v_g": ln_v_g, "ln_v_b": ln_v_b,
        "w_s": w_s.astype(BF16),
        "b_s": jnp.repeat(jnp.swapaxes(b_s, 1, 2), A_WIDTH // A_GROUPS, axis=-1),
        "conv_w": conv_w, "conv_b": conv_b,
        "w_ax": (0.5 * jnp.concatenate([w_a, w_x], axis=-1)).astype(BF16),
        "b_a": b_a, "b_x": b_x, "lam": lam,
        "p_a": p_a.astype(BF16), "p_b": p_b.astype(BF16), "p_c": p_c.astype(BF16),
        "w_o": w_o.astype(BF16), "b_o": b_o, "ln1_g": ln1_g, "ln1_b": ln1_b,
        "wr": jnp.swapaxes(jnp.concatenate([wr_hi, (wr - wr_hi.astype(F32)).astype(BF16)], axis=-1), 1, 2),
        "br": jnp.concatenate(
            [b_rg, b_re, jnp.zeros((DEPTH, ROUTER_LANES - N_GROUPS - N_EXPERTS), F32)], axis=-1)[:, :, None],
        "w_up": w_up, "w_down": w_down,
        "ln2_g": ln2_g, "ln2_b": ln2_b,
    }
    lni_g = ln_in_g.reshape(1, D_MODEL)
    lni_b = ln_in_b.reshape(1, D_MODEL)

    kt, vm = _prep_call(mem, ln_mem_g, ln_mem_b, w_kv.astype(BF16))
    for layer in range(DEPTH):
        x = _mixer_call(layer, x, lni_g, lni_b, kt, vm, w)
        x = _moe_call(layer, x.reshape(bsz * seq, D_MODEL), w).reshape(bsz, seq, D_MODEL)
    return x
```

```python
import functools
import math

import jax
import jax.numpy as jnp
from jax import lax
from jax.experimental import pallas as pl
from jax.experimental.pallas import tpu as pltpu

D_MODEL = 1024
DEPTH = 2
N_MEM = 256
CHUNK = 128
A_GROUPS = 8
A_WIDTH = 1024
B_HEADS = 10
B_WIDTH = 1280
B_HEAD_DIM = B_WIDTH // B_HEADS
CONV_WIDTH = 4
LRU_C = 8.0
C_HEADS = 4
C_HEAD_DIM = 256
C_WIDTH = C_HEADS * C_HEAD_DIM
IN_WIDTH = 2 * A_WIDTH + 2 * B_WIDTH + C_WIDTH + 3 * D_MODEL
OFF_U = 0
OFF_V = A_WIDTH
OFF_XB = 2 * A_WIDTH
OFF_GB = OFF_XB + B_WIDTH
OFF_Q = OFF_GB + B_WIDTH
OFF_M = OFF_Q + C_WIDTH
N_GROUPS = 4
EXPERTS_PER_GROUP = 4
N_EXPERTS = 16
D_EXPERT = 256
ALPHA = (2 * DEPTH) ** 0.25
LN_EPS = 1e-5
LOG2_E = 1.0 / math.log(2.0)
GELU_K1 = -2.0 * math.sqrt(2.0 / math.pi) * LOG2_E
GELU_K3 = 0.044715 * GELU_K1

SUBLANES = 8
LANES = 128
ROUTER_LANES = LANES
TOKEN_TILE = 512
MOE_TILE = 512
ROW_BLOCK = 128
MAX_BLOCKS = MOE_TILE // ROW_BLOCK + N_GROUPS - 1
ALWAYS_BLOCKS = MOE_TILE // ROW_BLOCK + 1
SORTED_ROWS = MAX_BLOCKS * ROW_BLOCK
FINISH_ROWS = 128
VMEM_LIMIT = 56 * 1024 * 1024

BF16 = jnp.bfloat16
F32 = jnp.float32


def _dot(a, b):
    return jnp.dot(a, b, preferred_element_type=F32)


def _layer_norm(x, g, b):
    mu = jnp.mean(x, axis=-1, keepdims=True)
    xc = x - mu
    var = jnp.mean(xc * xc, axis=-1, keepdims=True)
    return xc * lax.rsqrt(var + LN_EPS) * g + b


def _gelu(x):
    return x * (1.0 / (1.0 + jnp.exp2(x * (GELU_K1 + GELU_K3 * (x * x)))))


def _sigmoid_of_twice(h):
    return 0.5 + 0.5 * jnp.tanh(h)


def _silu(x):
    return x * (1.0 / (1.0 + jnp.exp2(-LOG2_E * x)))


def _sqrt_nonneg(y):
    return jnp.exp2(jnp.log(y) * (0.5 * LOG2_E))


def _prep_kernel(mem_ref, g_ref, b_ref, wkv_ref, kt_ref, v_ref):
    bsz = mem_ref.shape[0]
    mem = mem_ref[...].reshape(bsz * N_MEM, D_MODEL)
    mem_n = _layer_norm(mem, g_ref[...], b_ref[...]).astype(BF16)
    kv = _dot(mem_n, wkv_ref[...])
    for b in range(bsz):
        rows = slice(b * N_MEM, (b + 1) * N_MEM)
        kt_ref[b] = kv[rows, :C_WIDTH].T.astype(BF16)
        v_ref[b] = kv[rows, C_WIDTH:].astype(BF16)


def _prep_call(mem, ln_g, ln_b, w_kv_bf):
    bsz = mem.shape[0]
    return pl.pallas_call(
        _prep_kernel,
        grid=(DEPTH,),
        in_specs=[
            pl.BlockSpec((bsz, N_MEM, D_MODEL), lambda l: (0, 0, 0)),
            pl.BlockSpec((1, D_MODEL), lambda l: (0, 0)),
            pl.BlockSpec((1, D_MODEL), lambda l: (0, 0)),
            pl.BlockSpec((None, D_MODEL, 2 * C_WIDTH), lambda l: (l, 0, 0)),
        ],
        out_specs=[
            pl.BlockSpec((None, bsz, C_WIDTH, N_MEM), lambda l: (l, 0, 0, 0)),
            pl.BlockSpec((None, bsz, N_MEM, C_WIDTH), lambda l: (l, 0, 0, 0)),
        ],
        out_shape=[
            jax.ShapeDtypeStruct((DEPTH, bsz, C_WIDTH, N_MEM), BF16),
            jax.ShapeDtypeStruct((DEPTH, bsz, N_MEM, C_WIDTH), BF16),
        ],
        compiler_params=pltpu.CompilerParams(
            dimension_semantics=("arbitrary",), vmem_limit_bytes=VMEM_LIMIT),
    )(mem, ln_g.reshape(1, D_MODEL), ln_b.reshape(1, D_MODEL), w_kv_bf)


def _mixer_kernel(layer, x_ref, lni_g_ref, lni_b_ref, kt_ref, vm_ref, w_in_ref, b_in_ref,
                  lnv_g_ref, lnv_b_ref, ws_ref, bs_ref, convw_ref, convb_ref, wax_ref, ba_ref, bx_ref,
                  lam_ref, pa_ref, pb_ref, pc_ref, wo_ref, bo_ref, ln1_g_ref, ln1_b_ref,
                  out_ref, seg_scr, tail_scr, hlast_scr):
    tm = x_ref.shape[0]
    pre_ln = layer == 0
    (b_in_ref, lnv_g_ref, lnv_b_ref, convb_ref, ba_ref, bx_ref, lam_ref, bo_ref, ln1_g_ref, ln1_b_ref) = (
        r.at[layer:layer + 1] for r in (b_in_ref, lnv_g_ref, lnv_b_ref, convb_ref, ba_ref, bx_ref, lam_ref,
                                        bo_ref, ln1_g_ref, ln1_b_ref))
    convw_ref = convw_ref.at[layer]

    @pl.when(pl.program_id(1) == 0)
    def _():
        tail_scr[...] = jnp.zeros(tail_scr.shape, F32)
        hlast_scr[...] = jnp.zeros((1, B_WIDTH), F32)

    x = x_ref[...]
    if pre_ln:
        x = _layer_norm(x, lni_g_ref[...], lni_b_ref[...])
    xb = x.astype(BF16)

    def proj(lo, width):
        bias = b_in_ref[:, lo:lo + width]
        if lo >= OFF_M:
            bias = 0.5 * bias
        return _dot(xb, w_in_ref[:, lo:lo + width]) + bias

    v_pre = proj(OFF_V, A_WIDTH)
    xbr = proj(OFF_XB, B_WIDTH)
    vn = _layer_norm(_gelu(v_pre), lnv_g_ref[...], lnv_b_ref[...]).astype(BF16)
    u_pre = proj(OFF_U, A_WIDTH)

    seg = tm // SUBLANES
    pitch = seg + SUBLANES
    n_slab = B_WIDTH // LANES
    sub = lax.broadcasted_iota(jnp.int32, (SUBLANES, B_WIDTH), 0)
    for s in range(SUBLANES):
        for l in range(n_slab):
            seg_scr[l, pitch * s:pitch * s + seg, :] = xbr[seg * s:seg * (s + 1), LANES * l:LANES * (l + 1)]
    xs = [jnp.concatenate([seg_scr[l, pl.ds(j, SUBLANES, stride=pitch), :] for l in range(n_slab)], axis=1)
          for j in range(seg)]

    def wrap(cur, prv):
        return pltpu.roll(jnp.where(sub == SUBLANES - 1, prv, cur), 1, 0)

    n_tail = CONV_WIDTH - 1
    ext = [wrap(xs[seg - n_tail + k], tail_scr[SUBLANES * k:SUBLANES * (k + 1), :]) for k in range(n_tail)] + xs
    for k in range(n_tail):
        tail_scr[SUBLANES * k:SUBLANES * (k + 1), :] = xs[seg - n_tail + k]
    conv_w = [jnp.broadcast_to(convw_ref[k:k + 1, :], (SUBLANES, B_WIDTH)) for k in range(CONV_WIDTH)]
    conv_b = jnp.broadcast_to(convb_ref[...], (SUBLANES, B_WIDTH))
    conv_steps = []
    for j in range(seg):
        acc = conv_b
        for k in range(CONV_WIDTH):
            acc = acc + conv_w[k] * ext[j + k]
        conv_steps.append(acc)
    conv = jnp.concatenate(conv_steps, axis=0)
    cb = conv.astype(BF16)
    gb_pre = proj(OFF_GB, B_WIDTH)
    u = _gelu(u_pre)

    row = lax.broadcasted_iota(jnp.int32, (CHUNK, CHUNK), 0)
    col = lax.broadcasted_iota(jnp.int32, (CHUNK, CHUNK), 1)
    causal = col <= row
    ws = [jnp.where(causal, ws_ref[g], jnp.zeros((CHUNK, CHUNK), BF16)) for g in range(A_GROUPS)]
    s_chunks = []
    for c in range(tm // CHUNK):
        vc = vn[c * CHUNK:(c + 1) * CHUNK, :]
        gd = A_WIDTH // A_GROUPS
        s_c = jnp.concatenate(
            [_dot(ws[g], vc[:, g * gd:(g + 1) * gd]) for g in range(A_GROUPS)], axis=1)
        s_chunks.append(s_c + bs_ref[...])
    gate_b = _gelu(gb_pre)
    q = proj(OFF_Q, C_WIDTH).astype(BF16)
    o_a = (u * jnp.concatenate(s_chunks, axis=0)).astype(BF16)
    ri = jnp.concatenate(
        [_dot(cb[:, h * B_HEAD_DIM:(h + 1) * B_HEAD_DIM], wax_ref[h]) for h in range(B_HEADS)],
        axis=1)
    m_a = proj(OFF_M, D_MODEL)

    r_parts = [ri[:, (2 * h) * B_HEAD_DIM:(2 * h + 1) * B_HEAD_DIM] for h in range(B_HEADS)]
    i_parts = [ri[:, (2 * h + 1) * B_HEAD_DIM:(2 * h + 2) * B_HEAD_DIM] for h in range(B_HEADS)]
    r = _sigmoid_of_twice(jnp.concatenate(r_parts, axis=1) + 0.5 * ba_ref[...])
    i = _sigmoid_of_twice(jnp.concatenate(i_parts, axis=1) + 0.5 * bx_ref[...])
    z = -lam_ref[...]
    softplus = jnp.maximum(z, 0.0) + jnp.log(1.0 + jnp.exp(-jnp.abs(z)))
    log_a = (-LRU_C * softplus) * r
    a = jnp.exp(log_a)
    bx = _sqrt_nonneg(1.0 - a * a) * (i * conv)
    a_steps = [a[SUBLANES * j:SUBLANES * (j + 1), :] for j in range(seg)]
    b_steps = [bx[SUBLANES * j:SUBLANES * (j + 1), :] for j in range(seg)]
    y_a = _dot(o_a, pa_ref[...])
    scores = [_dot(q[:, h * C_HEAD_DIM:(h + 1) * C_HEAD_DIM], kt_ref[h * C_HEAD_DIM:(h + 1) * C_HEAD_DIM, :])
              for h in range(C_HEADS)]

    h_loc, p_tot = b_steps[0], a_steps[0]
    for j in range(1, seg):
        h_loc = a_steps[j] * h_loc + b_steps[j]
        p_tot = p_tot * a_steps[j]
    for d in (1, 2, 4):
        m = sub >= d
        h_sh = jnp.where(m, pltpu.roll(h_loc, d, 0), 0.0)
        p_sh = jnp.where(m, pltpu.roll(p_tot, d, 0), 1.0)
        h_loc = p_tot * h_sh + h_loc
        p_tot = p_tot * p_sh
    h_last = hlast_scr[...]
    h_end = h_loc + p_tot * h_last
    hlast_scr[...] = h_end[SUBLANES - 1:SUBLANES, :]
    h = jnp.where(sub == 0, h_last, pltpu.roll(h_end, 1, 0))
    for j in range(seg):
        h = a_steps[j] * h + b_steps[j]
        for l in range(n_slab):
            seg_scr[l, pl.ds(j, SUBLANES, stride=pitch), :] = h[:, LANES * l:LANES * (l + 1)]
    h_tok = jnp.concatenate(
        [jnp.concatenate([seg_scr[l, pitch * s:pitch * s + seg, :] for l in range(n_slab)], axis=1)
         for s in range(SUBLANES)], axis=0)
    o_b = (h_tok * gate_b).astype(BF16)

    m_b = proj(OFF_M + D_MODEL, D_MODEL)
    probs, denoms = [], []
    for h in range(C_HEADS):
        s = scores[h] * (C_HEAD_DIM ** -0.5)
        p = jnp.exp(s - jnp.max(s, axis=-1, keepdims=True))
        denoms.append(jnp.sum(p, axis=-1, keepdims=True))
        probs.append(p.astype(BF16))
    m_c = proj(OFF_M + 2 * D_MODEL, D_MODEL)
    y_b = _dot(o_b, pb_ref[...])
    o_c = jnp.concatenate(
        [_dot(probs[h], vm_ref[:, h * C_HEAD_DIM:(h + 1) * C_HEAD_DIM]) / denoms[h] for h in range(C_HEADS)],
        axis=1).astype(BF16)
    y_ab = _sigmoid_of_twice(m_a) * y_a + _sigmoid_of_twice(m_b) * y_b
    y_c = _dot(o_c, pc_ref[...])

    y = y_ab + _sigmoid_of_twice(m_c) * y_c
    m = _dot(y.astype(BF16), wo_ref[...]) + bo_ref[...]
    out_ref[...] = _layer_norm(ALPHA * x + m, ln1_g_ref[...], ln1_b_ref[...])


def _resident(block_shape, index_map):
    return pl.BlockSpec(block_shape, index_map, pipeline_mode=pl.Buffered(1))


def _mixer_call(layer, x, lni_g, lni_b, kt, vm, w):
    bsz, seq, _ = x.shape
    tm = TOKEN_TILE

    def lw(*shape):
        return _resident((None,) + shape, lambda b, s: (layer,) + (0,) * len(shape))

    def vec(*shape):
        return _resident((DEPTH,) + shape, lambda b, s: (0,) * (1 + len(shape)))

    in_specs = [
        pl.BlockSpec((None, tm, D_MODEL), lambda b, s: (b, s, 0)),
        _resident((1, D_MODEL), lambda b, s: (0, 0)),
        _resident((1, D_MODEL), lambda b, s: (0, 0)),
        pl.BlockSpec((None, None, C_WIDTH, N_MEM), lambda b, s: (layer, b, 0, 0)),
        pl.BlockSpec((None, None, N_MEM, C_WIDTH), lambda b, s: (layer, b, 0, 0)),
        lw(D_MODEL, IN_WIDTH), vec(IN_WIDTH),
        vec(A_WIDTH), vec(A_WIDTH),
        lw(A_GROUPS, CHUNK, CHUNK), lw(CHUNK, A_WIDTH),
        vec(CONV_WIDTH, B_WIDTH), vec(B_WIDTH),
        lw(B_HEADS, B_HEAD_DIM, 2 * B_HEAD_DIM), vec(B_WIDTH), vec(B_WIDTH), vec(B_WIDTH),
        lw(A_WIDTH, D_MODEL), lw(B_WIDTH, D_MODEL), lw(C_WIDTH, D_MODEL),
        lw(D_MODEL, D_MODEL), vec(D_MODEL), vec(D_MODEL), vec(D_MODEL),
    ]
    return pl.pallas_call(
        functools.partial(_mixer_kernel, layer),
        grid=(bsz, seq // tm),
        in_specs=in_specs,
        out_specs=pl.BlockSpec((None, tm, D_MODEL), lambda b, s: (b, s, 0)),
        out_shape=jax.ShapeDtypeStruct(x.shape, F32),
        scratch_shapes=[
            pltpu.VMEM((B_WIDTH // LANES, tm + SUBLANES * SUBLANES, LANES), F32),
            pltpu.VMEM(((CONV_WIDTH - 1) * SUBLANES, B_WIDTH), F32),
            pltpu.VMEM((1, B_WIDTH), F32),
        ],
        compiler_params=pltpu.CompilerParams(
            dimension_semantics=("arbitrary", "arbitrary"), vmem_limit_bytes=VMEM_LIMIT),
    )(x, lni_g, lni_b, kt, vm, w["w_in"], w["b_in"], w["ln_v_g"], w["ln_v_b"], w["w_s"], w["b_s"],
      w["conv_w"], w["conv_b"], w["w_ax"], w["b_a"], w["b_x"], w["lam"], w["p_a"], w["p_b"], w["p_c"],
      w["w_o"], w["b_o"], w["ln1_g"], w["ln1_b"])


def _first_max_of4(c):
    m = jnp.maximum(jnp.maximum(c[0], c[1]), jnp.maximum(c[2], c[3]))
    idx = jnp.where(c[0] == m, 0, jnp.where(c[1] == m, 1, jnp.where(c[2] == m, 2, 3)))
    return m, idx


def _moe_kernel(layer, x_next_ref, x_ref, wr_ref, br_ref, wup_src_ref, wdn_src_ref, ln2_g_ref, ln2_b_ref,
                out_ref, wup_scr, wdn_scr, g2_scr, pt_scr, dest_scr, ends_scr, f_scr):
    step = pl.program_id(0)

    @pl.when(step < N_EXPERTS)
    def _():
        wup_scr[step] = wup_src_ref[...].astype(BF16)
        grp = lax.div(step, EXPERTS_PER_GROUP)
        row0 = pl.multiple_of(lax.rem(step, EXPERTS_PER_GROUP) * D_EXPERT, D_EXPERT)
        wdn_scr[grp, pl.ds(row0, D_EXPERT), :] = wdn_src_ref[...].astype(BF16)

    @pl.when(step == N_EXPERTS - 1)
    def _():
        _store_route(0, *_route(x_next_ref, wr_ref, br_ref), g2_scr, pt_scr, dest_scr, ends_scr)

    pl.when(step >= N_EXPERTS)(functools.partial(
        _moe_tile, layer, step - (N_EXPERTS - 1), x_next_ref, x_ref, wr_ref, br_ref, wup_scr, wdn_scr,
        ln2_g_ref, ln2_b_ref, out_ref, g2_scr, pt_scr, dest_scr, ends_scr, f_scr))


def _moe_tile(layer, step, x_next_ref, x_ref, wr_ref, br_ref, wup_ref, wdn_ref, ln2_g_ref, ln2_b_ref,
              out_ref, g2_scr, pt_scr, dest_scr, ends_scr, f_scr):
    new = lax.rem(step, 2)
    cur = 1 - new
    ln2_g_ref, ln2_b_ref = ln2_g_ref.at[layer:layer + 1], ln2_b_ref.at[layer:layer + 1]

    new_ends, dest_f, per_token = _route(x_next_ref, wr_ref, br_ref)
    _moe_experts_and_store(new, cur, new_ends, dest_f, per_token, x_ref, wup_ref, wdn_ref, ln2_g_ref, ln2_b_ref,
                           out_ref, g2_scr, pt_scr, dest_scr, ends_scr, f_scr)


def _route(x_next_ref, wr_ref, br_ref):
    tm = x_next_ref.shape[0]
    lgt = lax.dot_general(wr_ref[...], x_next_ref[...].astype(BF16), (((1,), (1,)), ((), ())),
                          preferred_element_type=F32)
    lt = lgt[:ROUTER_LANES, :] + lgt[ROUTER_LANES:, :] + br_ref[...]
    neg = -jnp.inf
    g_logit = [lt[g:g + 1, :] for g in range(N_GROUPS)]
    e_logit = [lt[N_GROUPS + e:N_GROUPS + e + 1, :] for e in range(N_EXPERTS)]

    gmax, g_top = _first_max_of4(g_logit)
    p_top = 1.0 / sum(jnp.exp(gl - gmax) for gl in g_logit)

    cand = [jnp.where(g_top == 0, e_logit[j], jnp.where(
        g_top == 1, e_logit[EXPERTS_PER_GROUP + j], jnp.where(
            g_top == 2, e_logit[2 * EXPERTS_PER_GROUP + j], e_logit[3 * EXPERTS_PER_GROUP + j])))
        for j in range(EXPERTS_PER_GROUP)]
    m1, i1 = _first_max_of4(cand)
    m2, i2 = _first_max_of4([jnp.where(i1 == j, neg, cand[j]) for j in range(EXPERTS_PER_GROUP)])
    e2 = jnp.exp(m2 - m1)
    w1 = p_top / (1.0 + e2)
    w2 = w1 * e2
    gates = [jnp.where(i1 == j, w1, 0.0) + jnp.where(i2 == j, w2, 0.0) for j in range(EXPERTS_PER_GROUP)]

    onehot8 = jnp.concatenate([jnp.where(g_top == g, 1.0, 0.0) for g in range(N_GROUPS)]
                              + [jnp.zeros((SUBLANES - N_GROUPS, tm), F32)], axis=0)
    upto = onehot8
    lane = lax.broadcasted_iota(jnp.int32, (SUBLANES, tm), 1)
    shift = 1
    while shift < tm:
        upto = upto + jnp.where(lane >= shift, pltpu.roll(upto, shift, 1), 0.0)
        shift *= 2
    rank = upto - onehot8
    dest_f = jnp.zeros((1, tm), F32)
    new_ends = []
    end = 0
    for g in range(N_GROUPS):
        count = upto[g:g + 1, tm - 1:tm][0, 0].astype(jnp.int32)
        base = end.astype(F32) if g else 0.0
        dest_f = dest_f + onehot8[g:g + 1, :] * (rank[g:g + 1, :] + base)
        n_blocks = lax.shift_right_logical(count + (ROW_BLOCK - 1), ROW_BLOCK.bit_length() - 1)
        end = end + n_blocks * ROW_BLOCK
        new_ends.append(end)
    per_token = jnp.concatenate(
        [dest_f] + gates + [jnp.zeros((LANES - 1 - EXPERTS_PER_GROUP, tm), F32)], axis=0).T
    return new_ends, dest_f, per_token


def _store_route(slot, new_ends, dest_f, per_token, g2_scr, pt_scr, dest_scr, ends_scr):
    tm = per_token.shape[0]
    for g in range(N_GROUPS):
        ends_scr[slot, g] = new_ends[g]
    dest_scr[slot] = jnp.broadcast_to(dest_f.astype(jnp.int32), (SUBLANES, tm))
    pt_scr[slot] = jnp.where(lax.broadcasted_iota(jnp.int32, (tm, SORTED_ROWS), 1)
                             == per_token[:, 0:1].astype(jnp.int32), 1.0, 0.0).astype(BF16)
    g_hi = per_token.astype(BF16)
    g2_scr[slot] = jnp.concatenate([g_hi, (per_token - g_hi.astype(F32)).astype(BF16)], axis=1)


def _moe_experts_and_store(new, cur, new_ends, dest_f, per_token, x_ref, wup_ref, wdn_ref, ln2_g_ref, ln2_b_ref,
                           out_ref, g2_scr, pt_scr, dest_scr, ends_scr, f_scr):
    tm = x_ref.shape[0]
    ends = [ends_scr[cur, g] for g in range(N_GROUPS)]
    dest_row = dest_scr[cur, 0:1, :]
    block_row = lax.broadcasted_iota(jnp.int32, (ROW_BLOCK, tm), 0)

    def expert_block(b):
        start = b * ROW_BLOCK
        grp = sum((start >= ends[g]).astype(jnp.int32) for g in range(N_GROUPS - 1))
        perm = jnp.where(block_row + start == dest_row, 1.0, 0.0).astype(BF16)
        xsb = _dot(perm, x_ref[...].astype(BF16)).astype(BF16)
        gs2 = _dot(perm, g2_scr[cur])
        gsb = gs2[:, :ROUTER_LANES] + gs2[:, ROUTER_LANES:]
        acts = []
        for j in range(EXPERTS_PER_GROUP):
            h = _dot(xsb, wup_ref[grp * EXPERTS_PER_GROUP + j])
            hg = h[:, :D_EXPERT]
            acts.append((_silu(hg) * h[:, D_EXPERT:] * gsb[:, j + 1:j + 2]).astype(BF16))
        f_scr[start:start + ROW_BLOCK, :] = _dot(jnp.concatenate(acts, axis=1), wdn_ref[grp]).astype(BF16)

    for b in range(ALWAYS_BLOCKS):
        expert_block(b)

    _store_route(new, new_ends, dest_f, per_token, g2_scr, pt_scr, dest_scr, ends_scr)

    for b in range(ALWAYS_BLOCKS, MAX_BLOCKS):
        used = b * ROW_BLOCK < ends[N_GROUPS - 1]
        pl.when(used)(functools.partial(expert_block, b))

        @pl.when(jnp.logical_not(used))
        def _():
            f_scr[b * ROW_BLOCK:(b + 1) * ROW_BLOCK, :] = jnp.zeros((ROW_BLOCK, D_MODEL), BF16)

    for r0 in range(0, tm, FINISH_ROWS):
        rows = slice(r0, r0 + FINISH_ROWS)
        f = _dot(pt_scr[cur, rows, :], f_scr[...])
        out_ref[rows, :] = _layer_norm(ALPHA * x_ref[rows, :] + f, ln2_g_ref[...], ln2_b_ref[...])


def _moe_call(layer, x2d, w):
    n_tok = x2d.shape[0]
    tm = MOE_TILE

    def lw(*shape):
        return _resident((None,) + shape, lambda t: (layer,) + (0,) * len(shape))

    n_tiles = n_tok // tm

    def routed(s):
        return jnp.clip(s - (N_EXPERTS - 1), 0, n_tiles - 1)

    def finished(s):
        return jnp.clip(s - N_EXPERTS, 0, n_tiles - 1)

    def expert(s):
        return jnp.minimum(s, N_EXPERTS - 1)

    return pl.pallas_call(
        functools.partial(_moe_kernel, layer),
        grid=(N_EXPERTS + n_tiles,),
        in_specs=[
            pl.BlockSpec((tm, D_MODEL), lambda s: (routed(s), 0)),
            pl.BlockSpec((tm, D_MODEL), lambda s: (finished(s), 0)),
            lw(2 * ROUTER_LANES, D_MODEL), lw(ROUTER_LANES, 1),
            pl.BlockSpec((None, None, D_MODEL, 2 * D_EXPERT), lambda s: (layer, expert(s), 0, 0)),
            pl.BlockSpec((None, None, D_EXPERT, D_MODEL), lambda s: (layer, expert(s), 0, 0)),
            _resident((DEPTH, D_MODEL), lambda s: (0, 0)), _resident((DEPTH, D_MODEL), lambda s: (0, 0)),
        ],
        out_specs=pl.BlockSpec((tm, D_MODEL), lambda s: (finished(s), 0)),
        out_shape=jax.ShapeDtypeStruct(x2d.shape, F32),
        scratch_shapes=[
            pltpu.VMEM((N_EXPERTS, D_MODEL, 2 * D_EXPERT), BF16),
            pltpu.VMEM((N_GROUPS, EXPERTS_PER_GROUP * D_EXPERT, D_MODEL), BF16),
            pltpu.VMEM((2, tm, 2 * ROUTER_LANES), BF16),
            pltpu.VMEM((2, tm, SORTED_ROWS), BF16),
            pltpu.VMEM((2, SUBLANES, tm), jnp.int32),
            pltpu.SMEM((2, N_GROUPS), jnp.int32),
            pltpu.VMEM((SORTED_ROWS, D_MODEL), BF16),
        ],
        compiler_params=pltpu.CompilerParams(
            dimension_semantics=("arbitrary",), vmem_limit_bytes=VMEM_LIMIT),
    )(x2d, x2d, w["wr"], w["br"], w["w_up"], w["w_down"], w["ln2_g"], w["ln2_b"])


def kernel(x, mem, ln_in_g, ln_in_b, ln_mem_g, ln_mem_b, w_in, b_in, ln_v_g, ln_v_b, w_s, b_s, conv_w, conv_b, w_a, b_a, w_x, b_x, lam, w_kv, p_a, p_b, p_c, w_o, b_o, ln1_g, ln1_b, w_rg, b_rg, w_re, b_re, w_up, w_down, ln2_g, ln2_b):
    bsz, seq, _ = x.shape
    assert seq % TOKEN_TILE == 0 and TOKEN_TILE % CHUNK == 0 and (bsz * seq) % MOE_TILE == 0

    wr = jnp.concatenate(
        [w_rg, w_re, jnp.zeros((DEPTH, D_MODEL, ROUTER_LANES - N_GROUPS - N_EXPERTS), F32)], axis=-1)
    wr_hi = wr.astype(BF16)
    in_scale = jnp.where(jnp.arange(IN_WIDTH) >= OFF_M, 0.5, 1.0).astype(F32)
    w = {
        "w_in": (w_in * in_scale).astype(BF16), "b_in": b_in,
        "ln_v_g": ln_v_g, "ln_v_b": ln_v_b,
        "w_s": w_s.astype(BF16),
        "b_s": jnp.repeat(jnp.swapaxes(b_s, 1, 2), A_WIDTH // A_GROUPS, axis=-1),
        "conv_w": conv_w, "conv_b": conv_b,
        "w_ax": (0.5 * jnp.concatenate([w_a, w_x], axis=-1)).astype(BF16),
        "b_a": b_a, "b_x": b_x, "lam": lam,
        "p_a": p_a.astype(BF16), "p_b": p_b.astype(BF16), "p_c": p_c.astype(BF16),
        "w_o": w_o.astype(BF16), "b_o": b_o, "ln1_g": ln1_g, "ln1_b": ln1_b,
        "wr": jnp.swapaxes(jnp.concatenate([wr_hi, (wr - wr_hi.astype(F32)).astype(BF16)], axis=-1), 1, 2),
        "br": jnp.concatenate(
            [b_rg, b_re, jnp.zeros((DEPTH, ROUTER_LANES - N_GROUPS - N_EXPERTS), F32)], axis=-1)[:, :, None],
        "w_up": w_up, "w_down": w_down,
        "ln2_g": ln2_g, "ln2_b": ln2_b,
    }
    lni_g = ln_in_g.reshape(1, D_MODEL)
    lni_b = ln_in_b.reshape(1, D_MODEL)

    kt, vm = _prep_call(mem, ln_mem_g, ln_mem_b, w_kv.astype(BF16))
    for layer in range(DEPTH):
        x = _mixer_call(layer, x, lni_g, lni_b, kt, vm, w)
        x = _moe_call(layer, x.reshape(bsz * seq, D_MODEL), w).reshape(bsz, seq, D_MODEL)
    return x
```

```python
import functools
import math

import jax
import jax.numpy as jnp
from jax import lax
from jax.experimental import pallas as pl
from jax.experimental.pallas import tpu as pltpu

D_MODEL = 1024
DEPTH = 2
N_MEM = 256
CHUNK = 128
A_GROUPS = 8
A_WIDTH = 1024
B_HEADS = 10
B_WIDTH = 1280
B_HEAD_DIM = B_WIDTH // B_HEADS
CONV_WIDTH = 4
LRU_C = 8.0
C_HEADS = 4
C_HEAD_DIM = 256
C_WIDTH = C_HEADS * C_HEAD_DIM
IN_WIDTH = 2 * A_WIDTH + 2 * B_WIDTH + C_WIDTH + 3 * D_MODEL
OFF_U = 0
OFF_V = A_WIDTH
OFF_XB = 2 * A_WIDTH
OFF_GB = OFF_XB + B_WIDTH
OFF_Q = OFF_GB + B_WIDTH
OFF_M = OFF_Q + C_WIDTH
N_GROUPS = 4
EXPERTS_PER_GROUP = 4
N_EXPERTS = 16
D_EXPERT = 256
ALPHA = (2 * DEPTH) ** 0.25
LN_EPS = 1e-5
LOG2_E = 1.0 / math.log(2.0)
GELU_K1 = -2.0 * math.sqrt(2.0 / math.pi) * LOG2_E
GELU_K3 = 0.044715 * GELU_K1

SUBLANES = 8
LANES = 128
ROUTER_LANES = LANES
TOKEN_TILE = 512
MOE_TILE = 512
ROW_BLOCK = 128
MAX_BLOCKS = MOE_TILE // ROW_BLOCK + N_GROUPS - 1
ALWAYS_BLOCKS = MOE_TILE // ROW_BLOCK + 1
SORTED_ROWS = MAX_BLOCKS * ROW_BLOCK
FINISH_ROWS = 128
VMEM_LIMIT = 56 * 1024 * 1024

BF16 = jnp.bfloat16
F32 = jnp.float32


def _dot(a, b):
    return jnp.dot(a, b, preferred_element_type=F32)


def _layer_norm(x, g, b):
    mu = jnp.mean(x, axis=-1, keepdims=True)
    xc = x - mu
    var = jnp.mean(xc * xc, axis=-1, keepdims=True)
    return xc * lax.rsqrt(var + LN_EPS) * g + b


def _gelu(x):
    return x * (1.0 / (1.0 + jnp.exp2(x * (GELU_K1 + GELU_K3 * (x * x)))))


def _sigmoid_of_twice(h):
    return 0.5 + 0.5 * jnp.tanh(h)


def _silu(x):
    return x * (1.0 / (1.0 + jnp.exp2(-LOG2_E * x)))


def _sqrt_nonneg(y):
    return jnp.exp2(jnp.log(y) * (0.5 * LOG2_E))


def _prep_kernel(mem_ref, g_ref, b_ref, wkv_ref, kt_ref, v_ref):
    bsz = mem_ref.shape[0]
    mem = mem_ref[...].reshape(bsz * N_MEM, D_MODEL)
    mem_n = _layer_norm(mem, g_ref[...], b_ref[...]).astype(BF16)
    kv = _dot(mem_n, wkv_ref[...])
    for b in range(bsz):
        rows = slice(b * N_MEM, (b + 1) * N_MEM)
        kt_ref[b] = kv[rows, :C_WIDTH].T.astype(BF16)
        v_ref[b] = kv[rows, C_WIDTH:].astype(BF16)


def _prep_call(mem, ln_g, ln_b, w_kv_bf):
    bsz = mem.shape[0]
    return pl.pallas_call(
        _prep_kernel,
        grid=(DEPTH,),
        in_specs=[
            pl.BlockSpec((bsz, N_MEM, D_MODEL), lambda l: (0, 0, 0)),
            pl.BlockSpec((1, D_MODEL), lambda l: (0, 0)),
            pl.BlockSpec((1, D_MODEL), lambda l: (0, 0)),
            pl.BlockSpec((None, D_MODEL, 2 * C_WIDTH), lambda l: (l, 0, 0)),
        ],
        out_specs=[
            pl.BlockSpec((None, bsz, C_WIDTH, N_MEM), lambda l: (l, 0, 0, 0)),
            pl.BlockSpec((None, bsz, N_MEM, C_WIDTH), lambda l: (l, 0, 0, 0)),
        ],
        out_shape=[
            jax.ShapeDtypeStruct((DEPTH, bsz, C_WIDTH, N_MEM), BF16),
            jax.ShapeDtypeStruct((DEPTH, bsz, N_MEM, C_WIDTH), BF16),
        ],
        compiler_params=pltpu.CompilerParams(
            dimension_semantics=("arbitrary",), vmem_limit_bytes=VMEM_LIMIT),
    )(mem, ln_g.reshape(1, D_MODEL), ln_b.reshape(1, D_MODEL), w_kv_bf)


def _mixer_kernel(layer, x_ref, lni_g_ref, lni_b_ref, kt_ref, vm_ref, w_in_ref, b_in_ref,
                  lnv_g_ref, lnv_b_ref, ws_ref, bs_ref, convw_ref, convb_ref, wax_ref, ba_ref, bx_ref,
                  lam_ref, pa_ref, pb_ref, pc_ref, wo_ref, bo_ref, ln1_g_ref, ln1_b_ref,
                  out_ref, seg_scr, tail_scr, hlast_scr):
    tm = x_ref.shape[0]
    pre_ln = layer == 0
    (b_in_ref, lnv_g_ref, lnv_b_ref, convb_ref, ba_ref, bx_ref, lam_ref, bo_ref, ln1_g_ref, ln1_b_ref) = (
        r.at[layer:layer + 1] for r in (b_in_ref, lnv_g_ref, lnv_b_ref, convb_ref, ba_ref, bx_ref, lam_ref,
                                        bo_ref, ln1_g_ref, ln1_b_ref))
    convw_ref = convw_ref.at[layer]

    @pl.when(pl.program_id(1) == 0)
    def _():
        tail_scr[...] = jnp.zeros(tail_scr.shape, F32)
        hlast_scr[...] = jnp.zeros((1, B_WIDTH), F32)

    x = x_ref[...]
    if pre_ln:
        x = _layer_norm(x, lni_g_ref[...], lni_b_ref[...])
    xb = x.astype(BF16)

    def proj(lo, width):
        bias = b_in_ref[:, lo:lo + width]
        if lo >= OFF_M:
            bias = 0.5 * bias
        return _dot(xb, w_in_ref[:, lo:lo + width]) + bias

    v_pre = proj(OFF_V, A_WIDTH)
    xbr = proj(OFF_XB, B_WIDTH)
    vn = _layer_norm(_gelu(v_pre), lnv_g_ref[...], lnv_b_ref[...]).astype(BF16)
    u_pre = proj(OFF_U, A_WIDTH)

    seg = tm // SUBLANES
    pitch = seg + SUBLANES
    n_slab = B_WIDTH // LANES
    sub = lax.broadcasted_iota(jnp.int32, (SUBLANES, B_WIDTH), 0)
    for s in range(SUBLANES):
        for l in range(n_slab):
            seg_scr[l, pitch * s:pitch * s + seg, :] = xbr[seg * s:seg * (s + 1), LANES * l:LANES * (l + 1)]
    xs = [jnp.concatenate([seg_scr[l, pl.ds(j, SUBLANES, stride=pitch), :] for l in range(n_slab)], axis=1)
          for j in range(seg)]

    def wrap(cur, prv):
        return pltpu.roll(jnp.where(sub == SUBLANES - 1, prv, cur), 1, 0)

    n_tail = CONV_WIDTH - 1
    ext = [wrap(xs[seg - n_tail + k], tail_scr[SUBLANES * k:SUBLANES * (k + 1), :]) for k in range(n_tail)] + xs
    for k in range(n_tail):
        tail_scr[SUBLANES * k:SUBLANES * (k + 1), :] = xs[seg - n_tail + k]
    conv_w = [jnp.broadcast_to(convw_ref[k:k + 1, :], (SUBLANES, B_WIDTH)) for k in range(CONV_WIDTH)]
    conv_b = jnp.broadcast_to(convb_ref[...], (SUBLANES, B_WIDTH))
    conv_steps = []
    for j in range(seg):
        acc = conv_b
        for k in range(CONV_WIDTH):
            acc = acc + conv_w[k] * ext[j + k]
        conv_steps.append(acc)
    conv = jnp.concatenate(conv_steps, axis=0)
    cb = conv.astype(BF16)
    gb_pre = proj(OFF_GB, B_WIDTH)
    u = _gelu(u_pre).astype(BF16)

    row = lax.broadcasted_iota(jnp.int32, (CHUNK, CHUNK), 0)
    col = lax.broadcasted_iota(jnp.int32, (CHUNK, CHUNK), 1)
    causal = col <= row
    ws = [jnp.where(causal, ws_ref[g], jnp.zeros((CHUNK, CHUNK), BF16)) for g in range(A_GROUPS)]
    s_chunks = []
    for c in range(tm // CHUNK):
        vc = vn[c * CHUNK:(c + 1) * CHUNK, :]
        gd = A_WIDTH // A_GROUPS
        s_c = jnp.concatenate(
            [_dot(ws[g], vc[:, g * gd:(g + 1) * gd]) for g in range(A_GROUPS)], axis=1)
        s_chunks.append(s_c + bs_ref[...])
    gate_b = _gelu(gb_pre).astype(BF16)
    q = proj(OFF_Q, C_WIDTH).astype(BF16)
    o_a = (u * jnp.concatenate(s_chunks, axis=0)).astype(BF16)
    ri = jnp.concatenate(
        [_dot(cb[:, h * B_HEAD_DIM:(h + 1) * B_HEAD_DIM], wax_ref[h]) for h in range(B_HEADS)],
        axis=1)
    g_a = _sigmoid_of_twice(proj(OFF_M, D_MODEL)).astype(BF16)

    r_parts = [ri[:, (2 * h) * B_HEAD_DIM:(2 * h + 1) * B_HEAD_DIM] for h in range(B_HEADS)]
    i_parts = [ri[:, (2 * h + 1) * B_HEAD_DIM:(2 * h + 2) * B_HEAD_DIM] for h in range(B_HEADS)]
    r = _sigmoid_of_twice(jnp.concatenate(r_parts, axis=1) + 0.5 * ba_ref[...])
    i = _sigmoid_of_twice(jnp.concatenate(i_parts, axis=1) + 0.5 * bx_ref[...])
    z = -lam_ref[...]
    softplus = jnp.maximum(z, 0.0) + jnp.log(1.0 + jnp.exp(-jnp.abs(z)))
    log_a = (-LRU_C * softplus) * r
    a = jnp.exp(log_a)
    bx = _sqrt_nonneg(1.0 - a * a) * (i * conv)
    a_steps = [a[SUBLANES * j:SUBLANES * (j + 1), :] for j in range(seg)]
    b_steps = [bx[SUBLANES * j:SUBLANES * (j + 1), :] for j in range(seg)]
    y_a = _dot(o_a, pa_ref[...])
    scores = [_dot(q[:, h * C_HEAD_DIM:(h + 1) * C_HEAD_DIM], kt_ref[h * C_HEAD_DIM:(h + 1) * C_HEAD_DIM, :])
              for h in range(C_HEADS)]

    h_loc, p_tot = b_steps[0], a_steps[0]
    for j in range(1, seg):
        h_loc = a_steps[j] * h_loc + b_steps[j]
        p_tot = p_tot * a_steps[j]
    for d in (1, 2, 4):
        m = sub >= d
        h_sh = jnp.where(m, pltpu.roll(h_loc, d, 0), 0.0)
        p_sh = jnp.where(m, pltpu.roll(p_tot, d, 0), 1.0)
        h_loc = p_tot * h_sh + h_loc
        p_tot = p_tot * p_sh
    h_last = hlast_scr[...]
    h_end = h_loc + p_tot * h_last
    hlast_scr[...] = h_end[SUBLANES - 1:SUBLANES, :]
    h = jnp.where(sub == 0, h_last, pltpu.roll(h_end, 1, 0))
    for j in range(seg):
        h = a_steps[j] * h + b_steps[j]
        for l in range(n_slab):
            seg_scr[l, pl.ds(j, SUBLANES, stride=pitch), :] = h[:, LANES * l:LANES * (l + 1)]
    h_tok = jnp.concatenate(
        [jnp.concatenate([seg_scr[l, pitch * s:pitch * s + seg, :] for l in range(n_slab)], axis=1)
         for s in range(SUBLANES)], axis=0)
    o_b = (h_tok * gate_b).astype(BF16)

    g_b = _sigmoid_of_twice(proj(OFF_M + D_MODEL, D_MODEL)).astype(BF16)
    probs, denoms = [], []
    for h in range(C_HEADS):
        s = scores[h] * (C_HEAD_DIM ** -0.5)
        p = jnp.exp(s - jnp.max(s, axis=-1, keepdims=True))
        denoms.append(jnp.sum(p, axis=-1, keepdims=True))
        probs.append(p.astype(BF16))
    g_c = _sigmoid_of_twice(proj(OFF_M + 2 * D_MODEL, D_MODEL)).astype(BF16)
    y_b = _dot(o_b, pb_ref[...])
    o_c = jnp.concatenate(
        [_dot(probs[h], vm_ref[:, h * C_HEAD_DIM:(h + 1) * C_HEAD_DIM]) / denoms[h] for h in range(C_HEADS)],
        axis=1).astype(BF16)
    y_ab = g_a * y_a + g_b * y_b
    y_c = _dot(o_c, pc_ref[...])

    y = y_ab + g_c * y_c
    m = _dot(y.astype(BF16), wo_ref[...]) + bo_ref[...]
    out_ref[...] = _layer_norm(ALPHA * x + m, ln1_g_ref[...], ln1_b_ref[...])


def _resident(block_shape, index_map):
    return pl.BlockSpec(block_shape, index_map, pipeline_mode=pl.Buffered(1))


def _mixer_call(layer, x, lni_g, lni_b, kt, vm, w):
    bsz, seq, _ = x.shape
    tm = TOKEN_TILE

    def lw(*shape):
        return _resident((None,) + shape, lambda b, s: (layer,) + (0,) * len(shape))

    def vec(*shape):
        return _resident((DEPTH,) + shape, lambda b, s: (0,) * (1 + len(shape)))

    in_specs = [
        pl.BlockSpec((None, tm, D_MODEL), lambda b, s: (b, s, 0)),
        _resident((1, D_MODEL), lambda b, s: (0, 0)),
        _resident((1, D_MODEL), lambda b, s: (0, 0)),
        pl.BlockSpec((None, None, C_WIDTH, N_MEM), lambda b, s: (layer, b, 0, 0)),
        pl.BlockSpec((None, None, N_MEM, C_WIDTH), lambda b, s: (layer, b, 0, 0)),
        lw(D_MODEL, IN_WIDTH), vec(IN_WIDTH),
        vec(A_WIDTH), vec(A_WIDTH),
        lw(A_GROUPS, CHUNK, CHUNK), lw(CHUNK, A_WIDTH),
        vec(CONV_WIDTH, B_WIDTH), vec(B_WIDTH),
        lw(B_HEADS, B_HEAD_DIM, 2 * B_HEAD_DIM), vec(B_WIDTH), vec(B_WIDTH), vec(B_WIDTH),
        lw(A_WIDTH, D_MODEL), lw(B_WIDTH, D_MODEL), lw(C_WIDTH, D_MODEL),
        lw(D_MODEL, D_MODEL), vec(D_MODEL), vec(D_MODEL), vec(D_MODEL),
    ]
    return pl.pallas_call(
        functools.partial(_mixer_kernel, layer),
        grid=(bsz, seq // tm),
        in_specs=in_specs,
        out_specs=pl.BlockSpec((None, tm, D_MODEL), lambda b, s: (b, s, 0)),
        out_shape=jax.ShapeDtypeStruct(x.shape, F32),
        scratch_shapes=[
            pltpu.VMEM((B_WIDTH // LANES, tm + SUBLANES * SUBLANES, LANES), F32),
            pltpu.VMEM(((CONV_WIDTH - 1) * SUBLANES, B_WIDTH), F32),
            pltpu.VMEM((1, B_WIDTH), F32),
        ],
        compiler_params=pltpu.CompilerParams(
            dimension_semantics=("arbitrary", "arbitrary"), vmem_limit_bytes=VMEM_LIMIT),
    )(x, lni_g, lni_b, kt, vm, w["w_in"], w["b_in"], w["ln_v_g"], w["ln_v_b"], w["w_s"], w["b_s"],
      w["conv_w"], w["conv_b"], w["w_ax"], w["b_a"], w["b_x"], w["lam"], w["p_a"], w["p_b"], w["p_c"],
      w["w_o"], w["b_o"], w["ln1_g"], w["ln1_b"])


def _first_max_of4(c):
    m = jnp.maximum(jnp.maximum(c[0], c[1]), jnp.maximum(c[2], c[3]))
    idx = jnp.where(c[0] == m, 0, jnp.where(c[1] == m, 1, jnp.where(c[2] == m, 2, 3)))
    return m, idx


def _moe_kernel(layer, x_next_ref, x_ref, wr_ref, br_ref, wup_src_ref, wdn_src_ref, ln2_g_ref, ln2_b_ref,
                out_ref, wup_scr, wdn_scr, g2_scr, pt_scr, dest_scr, ends_scr, f_scr):
    step = pl.program_id(0)

    @pl.when(step < N_EXPERTS)
    def _():
        wup_scr[step] = wup_src_ref[...].astype(BF16)
        grp = lax.div(step, EXPERTS_PER_GROUP)
        row0 = pl.multiple_of(lax.rem(step, EXPERTS_PER_GROUP) * D_EXPERT, D_EXPERT)
        wdn_scr[grp, pl.ds(row0, D_EXPERT), :] = wdn_src_ref[...].astype(BF16)

    @pl.when(step == N_EXPERTS - 1)
    def _():
        _store_route(0, *_route(x_next_ref, wr_ref, br_ref), g2_scr, pt_scr, dest_scr, ends_scr)

    pl.when(step >= N_EXPERTS)(functools.partial(
        _moe_tile, layer, step - (N_EXPERTS - 1), x_next_ref, x_ref, wr_ref, br_ref, wup_scr, wdn_scr,
        ln2_g_ref, ln2_b_ref, out_ref, g2_scr, pt_scr, dest_scr, ends_scr, f_scr))


def _moe_tile(layer, step, x_next_ref, x_ref, wr_ref, br_ref, wup_ref, wdn_ref, ln2_g_ref, ln2_b_ref,
              out_ref, g2_scr, pt_scr, dest_scr, ends_scr, f_scr):
    new = lax.rem(step, 2)
    cur = 1 - new
    ln2_g_ref, ln2_b_ref = ln2_g_ref.at[layer:layer + 1], ln2_b_ref.at[layer:layer + 1]

    new_ends, dest_f, per_token = _route(x_next_ref, wr_ref, br_ref)
    _moe_experts_and_store(new, cur, new_ends, dest_f, per_token, x_ref, wup_ref, wdn_ref, ln2_g_ref, ln2_b_ref,
                           out_ref, g2_scr, pt_scr, dest_scr, ends_scr, f_scr)


def _route(x_next_ref, wr_ref, br_ref):
    tm = x_next_ref.shape[0]
    lgt = lax.dot_general(wr_ref[...], x_next_ref[...].astype(BF16), (((1,), (1,)), ((), ())),
                          preferred_element_type=F32)
    lt = lgt[:ROUTER_LANES, :] + lgt[ROUTER_LANES:, :] + br_ref[...]
    neg = -jnp.inf
    g_logit = [lt[g:g + 1, :] for g in range(N_GROUPS)]
    e_logit = [lt[N_GROUPS + e:N_GROUPS + e + 1, :] for e in range(N_EXPERTS)]

    gmax, g_top = _first_max_of4(g_logit)
    p_top = 1.0 / sum(jnp.exp(gl - gmax) for gl in g_logit)

    cand = [jnp.where(g_top == 0, e_logit[j], jnp.where(
        g_top == 1, e_logit[EXPERTS_PER_GROUP + j], jnp.where(
            g_top == 2, e_logit[2 * EXPERTS_PER_GROUP + j], e_logit[3 * EXPERTS_PER_GROUP + j])))
        for j in range(EXPERTS_PER_GROUP)]
    m1, i1 = _first_max_of4(cand)
    m2, i2 = _first_max_of4([jnp.where(i1 == j, neg, cand[j]) for j in range(EXPERTS_PER_GROUP)])
    e2 = jnp.exp(m2 - m1)
    w1 = p_top / (1.0 + e2)
    w2 = w1 * e2
    gates = [jnp.where(i1 == j, w1, 0.0) + jnp.where(i2 == j, w2, 0.0) for j in range(EXPERTS_PER_GROUP)]

    onehot8 = jnp.concatenate([jnp.where(g_top == g, 1.0, 0.0) for g in range(N_GROUPS)]
                              + [jnp.zeros((SUBLANES - N_GROUPS, tm), F32)], axis=0)
    upto = onehot8
    lane = lax.broadcasted_iota(jnp.int32, (SUBLANES, tm), 1)
    shift = 1
    while shift < tm:
        upto = upto + jnp.where(lane >= shift, pltpu.roll(upto, shift, 1), 0.0)
        shift *= 2
    rank = upto - onehot8
    dest_f = jnp.zeros((1, tm), F32)
    new_ends = []
    end = 0
    for g in range(N_GROUPS):
        count = upto[g:g + 1, tm - 1:tm][0, 0].astype(jnp.int32)
        base = end.astype(F32) if g else 0.0
        dest_f = dest_f + onehot8[g:g + 1, :] * (rank[g:g + 1, :] + base)
        n_blocks = lax.shift_right_logical(count + (ROW_BLOCK - 1), ROW_BLOCK.bit_length() - 1)
        end = end + n_blocks * ROW_BLOCK
        new_ends.append(end)
    per_token = jnp.concatenate(
        [dest_f] + gates + [jnp.zeros((LANES - 1 - EXPERTS_PER_GROUP, tm), F32)], axis=0).T
    return new_ends, dest_f, per_token


def _store_route(slot, new_ends, dest_f, per_token, g2_scr, pt_scr, dest_scr, ends_scr):
    tm = per_token.shape[0]
    for g in range(N_GROUPS):
        ends_scr[slot, g] = new_ends[g]
    dest_scr[slot] = jnp.broadcast_to(dest_f.astype(jnp.int32), (SUBLANES, tm))
    pt_scr[slot] = jnp.where(lax.broadcasted_iota(jnp.int32, (tm, SORTED_ROWS), 1)
                             == per_token[:, 0:1].astype(jnp.int32), 1.0, 0.0).astype(BF16)
    g_hi = per_token.astype(BF16)
    g2_scr[slot] = jnp.concatenate([g_hi, (per_token - g_hi.astype(F32)).astype(BF16)], axis=1)


def _moe_experts_and_store(new, cur, new_ends, dest_f, per_token, x_ref, wup_ref, wdn_ref, ln2_g_ref, ln2_b_ref,
                           out_ref, g2_scr, pt_scr, dest_scr, ends_scr, f_scr):
    tm = x_ref.shape[0]
    ends = [ends_scr[cur, g] for g in range(N_GROUPS)]
    dest_row = dest_scr[cur, 0:1, :]
    block_row = lax.broadcasted_iota(jnp.int32, (ROW_BLOCK, tm), 0)

    def expert_block(b):
        start = b * ROW_BLOCK
        grp = sum((start >= ends[g]).astype(jnp.int32) for g in range(N_GROUPS - 1))
        perm = jnp.where(block_row + start == dest_row, 1.0, 0.0).astype(BF16)
        xsb = _dot(perm, x_ref[...].astype(BF16)).astype(BF16)
        gs2 = _dot(perm, g2_scr[cur])
        gsb = gs2[:, :ROUTER_LANES] + gs2[:, ROUTER_LANES:]
        acts = []
        for j in range(EXPERTS_PER_GROUP):
            h = _dot(xsb, wup_ref[grp * EXPERTS_PER_GROUP + j])
            hg = h[:, :D_EXPERT]
            acts.append((_silu(hg) * h[:, D_EXPERT:] * gsb[:, j + 1:j + 2]).astype(BF16))
        f_scr[start:start + ROW_BLOCK, :] = _dot(jnp.concatenate(acts, axis=1), wdn_ref[grp]).astype(BF16)

    for b in range(ALWAYS_BLOCKS):
        expert_block(b)

    _store_route(new, new_ends, dest_f, per_token, g2_scr, pt_scr, dest_scr, ends_scr)

    for b in range(ALWAYS_BLOCKS, MAX_BLOCKS):
        used = b * ROW_BLOCK < ends[N_GROUPS - 1]
        pl.when(used)(functools.partial(expert_block, b))

        @pl.when(jnp.logical_not(used))
        def _():
            f_scr[b * ROW_BLOCK:(b + 1) * ROW_BLOCK, :] = jnp.zeros((ROW_BLOCK, D_MODEL), BF16)

    for r0 in range(0, tm, FINISH_ROWS):
        rows = slice(r0, r0 + FINISH_ROWS)
        f = _dot(pt_scr[cur, rows, :], f_scr[...])
        out_ref[rows, :] = _layer_norm(ALPHA * x_ref[rows, :] + f, ln2_g_ref[...], ln2_b_ref[...])


def _moe_call(layer, x2d, w):
    n_tok = x2d.shape[0]
    tm = MOE_TILE

    def lw(*shape):
        return _resident((None,) + shape, lambda t: (layer,) + (0,) * len(shape))

    n_tiles = n_tok // tm

    def routed(s):
        return jnp.clip(s - (N_EXPERTS - 1), 0, n_tiles - 1)

    def finished(s):
        return jnp.clip(s - N_EXPERTS, 0, n_tiles - 1)

    def expert(s):
        return jnp.minimum(s, N_EXPERTS - 1)

    return pl.pallas_call(
        functools.partial(_moe_kernel, layer),
        grid=(N_EXPERTS + n_tiles,),
        in_specs=[
            pl.BlockSpec((tm, D_MODEL), lambda s: (routed(s), 0)),
            pl.BlockSpec((tm, D_MODEL), lambda s: (finished(s), 0)),
            lw(2 * ROUTER_LANES, D_MODEL), lw(ROUTER_LANES, 1),
            pl.BlockSpec((None, None, D_MODEL, 2 * D_EXPERT), lambda s: (layer, expert(s), 0, 0)),
            pl.BlockSpec((None, None, D_EXPERT, D_MODEL), lambda s: (layer, expert(s), 0, 0)),
            _resident((DEPTH, D_MODEL), lambda s: (0, 0)), _resident((DEPTH, D_MODEL), lambda s: (0, 0)),
        ],
        out_specs=pl.BlockSpec((tm, D_MODEL), lambda s: (finished(s), 0)),
        out_shape=jax.ShapeDtypeStruct(x2d.shape, F32),
        scratch_shapes=[
            pltpu.VMEM((N_EXPERTS, D_MODEL, 2 * D_EXPERT), BF16),
            pltpu.VMEM((N_GROUPS, EXPERTS_PER_GROUP * D_EXPERT, D_MODEL), BF16),
            pltpu.VMEM((2, tm, 2 * ROUTER_LANES), BF16),
            pltpu.VMEM((2, tm, SORTED_ROWS), BF16),
            pltpu.VMEM((2, SUBLANES, tm), jnp.int32),
            pltpu.SMEM((2, N_GROUPS), jnp.int32),
            pltpu.VMEM((SORTED_ROWS, D_MODEL), BF16),
        ],
        compiler_params=pltpu.CompilerParams(
            dimension_semantics=("arbitrary",), vmem_limit_bytes=VMEM_LIMIT),
    )(x2d, x2d, w["wr"], w["br"], w["w_up"], w["w_down"], w["ln2_g"], w["ln2_b"])


def kernel(x, mem, ln_in_g, ln_in_b, ln_mem_g, ln_mem_b, w_in, b_in, ln_v_g, ln_v_b, w_s, b_s, conv_w, conv_b, w_a, b_a, w_x, b_x, lam, w_kv, p_a, p_b, p_c, w_o, b_o, ln1_g, ln1_b, w_rg, b_rg, w_re, b_re, w_up, w_down, ln2_g, ln2_b):
    bsz, seq, _ = x.shape
    assert seq % TOKEN_TILE == 0 and TOKEN_TILE % CHUNK == 0 and (bsz * seq) % MOE_TILE == 0

    wr = jnp.concatenate(
        [w_rg, w_re, jnp.zeros((DEPTH, D_MODEL, ROUTER_LANES - N_GROUPS - N_EXPERTS), F32)], axis=-1)
    wr_hi = wr.astype(BF16)
    in_scale = jnp.where(jnp.arange(IN_WIDTH) >= OFF_M, 0.5, 1.0).astype(F32)
    w = {
        "w_in": (w_in * in_scale).astype(BF16), "b_in": b_in,
        "ln_v_g": ln_v_g, "ln_v_b": ln_v_b,
        "w_s": w_s.astype(BF16),
        "b_s": jnp.repeat(jnp.swapaxes(b_s, 1, 2), A_WIDTH // A_GROUPS, axis=-1),
        "conv_w": conv_w, "conv_b": conv_b,
        "w_ax": (0.5 * jnp.concatenate([w_a, w_x], axis=-1)).astype(BF16),
        "b_a": b_a, "b_x": b_x, "lam": lam,
        "p_a": p_a.astype(BF16), "p_b": p_b.astype(BF16), "p_c": p_c.astype(BF16),
        "w_o": w_o.astype(BF16), "b_o": b_o, "ln1_g": ln1_g, "ln1_b": ln1_b,
        "wr": jnp.swapaxes(jnp.concatenate([wr_hi, (wr - wr_hi.astype(F32)).astype(BF16)], axis=-1), 1, 2),
        "br": jnp.concatenate(
            [b_rg, b_re, jnp.zeros((DEPTH, ROUTER_LANES - N_GROUPS - N_EXPERTS), F32)], axis=-1)[:, :, None],
        "w_up": w_up, "w_down": w_down,
        "ln2_g": ln2_g, "ln2_b": ln2_b,
    }
    lni_g = ln_in_g.reshape(1, D_MODEL)
    lni_b = ln_in_b.reshape(1, D_MODEL)

    kt, vm = _prep_call(mem, ln_mem_g, ln_mem_b, w_kv.astype(BF16))
    for layer in range(DEPTH):
        x = _mixer_call(layer, x, lni_g, lni_b, kt, vm, w)
        x = _moe_call(layer, x.reshape(bsz * seq, D_MODEL), w).reshape(bsz, seq, D_MODEL)
    return x
```

```python
import functools
import math

import jax
import jax.numpy as jnp
from jax import lax
from jax.experimental import pallas as pl
from jax.experimental.pallas import tpu as pltpu

D_MODEL = 1024
DEPTH = 2
N_MEM = 256
CHUNK = 128
A_GROUPS = 8
A_WIDTH = 1024
B_HEADS = 10
B_WIDTH = 1280
B_HEAD_DIM = B_WIDTH // B_HEADS
CONV_WIDTH = 4
LRU_C = 8.0
C_HEADS = 4
C_HEAD_DIM = 256
C_WIDTH = C_HEADS * C_HEAD_DIM
IN_WIDTH = 2 * A_WIDTH + 2 * B_WIDTH + C_WIDTH + 3 * D_MODEL
OFF_U = 0
OFF_V = A_WIDTH
OFF_XB = 2 * A_WIDTH
OFF_GB = OFF_XB + B_WIDTH
OFF_Q = OFF_GB + B_WIDTH
OFF_M = OFF_Q + C_WIDTH
N_GROUPS = 4
EXPERTS_PER_GROUP = 4
N_EXPERTS = 16
D_EXPERT = 256
ALPHA = (2 * DEPTH) ** 0.25
LN_EPS = 1e-5
LOG2_E = 1.0 / math.log(2.0)
GELU_K1 = -2.0 * math.sqrt(2.0 / math.pi) * LOG2_E
GELU_K3 = 0.044715 * GELU_K1

SUBLANES = 8
LANES = 128
ROUTER_LANES = LANES
TOKEN_TILE = 512
MOE_TILE = 512
ROW_BLOCK = 128
MAX_BLOCKS = MOE_TILE // ROW_BLOCK + N_GROUPS - 1
ALWAYS_BLOCKS = MOE_TILE // ROW_BLOCK + 1
SORTED_ROWS = MAX_BLOCKS * ROW_BLOCK
CAST_COLS = 256
FINISH_ROWS = 128
VMEM_LIMIT = 56 * 1024 * 1024

BF16 = jnp.bfloat16
F32 = jnp.float32


def _dot(a, b):
    return jnp.dot(a, b, preferred_element_type=F32)


def _layer_norm(x, g, b):
    mu = jnp.mean(x, axis=-1, keepdims=True)
    xc = x - mu
    var = jnp.mean(xc * xc, axis=-1, keepdims=True)
    return xc * lax.rsqrt(var + LN_EPS) * g + b


def _gelu(x):
    return x * (1.0 / (1.0 + jnp.exp2(x * (GELU_K1 + GELU_K3 * (x * x)))))


def _sigmoid_of_twice(h):
    return 0.5 + 0.5 * jnp.tanh(h)


def _silu(x):
    return x * (1.0 / (1.0 + jnp.exp2(-LOG2_E * x)))


def _sqrt_nonneg(y):
    return jnp.exp2(jnp.log(y) * (0.5 * LOG2_E))


def _prep_kernel(mem_ref, g_ref, b_ref, wkv_ref, kt_ref, v_ref):
    bsz = mem_ref.shape[0]
    mem = mem_ref[...].reshape(bsz * N_MEM, D_MODEL)
    mem_n = _layer_norm(mem, g_ref[...], b_ref[...]).astype(BF16)
    kv = _dot(mem_n, wkv_ref[...])
    for b in range(bsz):
        rows = slice(b * N_MEM, (b + 1) * N_MEM)
        kt_ref[b] = kv[rows, :C_WIDTH].T.astype(BF16)
        v_ref[b] = kv[rows, C_WIDTH:].astype(BF16)


def _prep_call(mem, ln_g, ln_b, w_kv_bf):
    bsz = mem.shape[0]
    return pl.pallas_call(
        _prep_kernel,
        grid=(DEPTH,),
        in_specs=[
            pl.BlockSpec((bsz, N_MEM, D_MODEL), lambda l: (0, 0, 0)),
            pl.BlockSpec((1, D_MODEL), lambda l: (0, 0)),
            pl.BlockSpec((1, D_MODEL), lambda l: (0, 0)),
            pl.BlockSpec((None, D_MODEL, 2 * C_WIDTH), lambda l: (l, 0, 0)),
        ],
        out_specs=[
            pl.BlockSpec((None, bsz, C_WIDTH, N_MEM), lambda l: (l, 0, 0, 0)),
            pl.BlockSpec((None, bsz, N_MEM, C_WIDTH), lambda l: (l, 0, 0, 0)),
        ],
        out_shape=[
            jax.ShapeDtypeStruct((DEPTH, bsz, C_WIDTH, N_MEM), BF16),
            jax.ShapeDtypeStruct((DEPTH, bsz, N_MEM, C_WIDTH), BF16),
        ],
        compiler_params=pltpu.CompilerParams(
            dimension_semantics=("arbitrary",), vmem_limit_bytes=VMEM_LIMIT),
    )(mem, ln_g.reshape(1, D_MODEL), ln_b.reshape(1, D_MODEL), w_kv_bf)


def _mixer_kernel(layer, x_ref, lni_g_ref, lni_b_ref, kt_ref, vm_ref, w_in_ref, b_in_ref,
                  lnv_g_ref, lnv_b_ref, ws_ref, bs_ref, convw_ref, convb_ref, wax_ref, ba_ref, bx_ref,
                  lam_ref, pa_ref, pb_ref, pc_ref, wo_ref, bo_ref, ln1_g_ref, ln1_b_ref,
                  out_ref, seg_scr, tail_scr, hlast_scr):
    tm = x_ref.shape[0]
    pre_ln = layer == 0
    (b_in_ref, lnv_g_ref, lnv_b_ref, convb_ref, ba_ref, bx_ref, lam_ref, bo_ref, ln1_g_ref, ln1_b_ref) = (
        r.at[layer:layer + 1] for r in (b_in_ref, lnv_g_ref, lnv_b_ref, convb_ref, ba_ref, bx_ref, lam_ref,
                                        bo_ref, ln1_g_ref, ln1_b_ref))
    convw_ref = convw_ref.at[layer]

    @pl.when(pl.program_id(1) == 0)
    def _():
        tail_scr[...] = jnp.zeros(tail_scr.shape, F32)
        hlast_scr[...] = jnp.zeros((1, B_WIDTH), F32)

    x = x_ref[...]
    if pre_ln:
        x = _layer_norm(x, lni_g_ref[...], lni_b_ref[...])
    xb = x.astype(BF16)

    def proj(lo, width):
        bias = b_in_ref[:, lo:lo + width]
        if lo >= OFF_M:
            bias = 0.5 * bias
        return _dot(xb, w_in_ref[:, lo:lo + width]) + bias

    v_pre = proj(OFF_V, A_WIDTH)
    xbr = proj(OFF_XB, B_WIDTH)
    vn = _layer_norm(_gelu(v_pre), lnv_g_ref[...], lnv_b_ref[...]).astype(BF16)
    u_pre = proj(OFF_U, A_WIDTH)

    seg = tm // SUBLANES
    pitch = seg + SUBLANES
    n_slab = B_WIDTH // LANES
    sub = lax.broadcasted_iota(jnp.int32, (SUBLANES, B_WIDTH), 0)
    for s in range(SUBLANES):
        for l in range(n_slab):
            seg_scr[l, pitch * s:pitch * s + seg, :] = xbr[seg * s:seg * (s + 1), LANES * l:LANES * (l + 1)]
    xs = [jnp.concatenate([seg_scr[l, pl.ds(j, SUBLANES, stride=pitch), :] for l in range(n_slab)], axis=1)
          for j in range(seg)]

    def wrap(cur, prv):
        return pltpu.roll(jnp.where(sub == SUBLANES - 1, prv, cur), 1, 0)

    n_tail = CONV_WIDTH - 1
    ext = [wrap(xs[seg - n_tail + k], tail_scr[SUBLANES * k:SUBLANES * (k + 1), :]) for k in range(n_tail)] + xs
    for k in range(n_tail):
        tail_scr[SUBLANES * k:SUBLANES * (k + 1), :] = xs[seg - n_tail + k]
    conv_w = [jnp.broadcast_to(convw_ref[k:k + 1, :], (SUBLANES, B_WIDTH)) for k in range(CONV_WIDTH)]
    conv_b = jnp.broadcast_to(convb_ref[...], (SUBLANES, B_WIDTH))
    conv_steps = []
    for j in range(seg):
        acc = conv_b
        for k in range(CONV_WIDTH):
            acc = acc + conv_w[k] * ext[j + k]
        conv_steps.append(acc)
    conv = jnp.concatenate(conv_steps, axis=0)
    cb = conv.astype(BF16)
    gb_pre = proj(OFF_GB, B_WIDTH)
    u = _gelu(u_pre)

    row = lax.broadcasted_iota(jnp.int32, (CHUNK, CHUNK), 0)
    col = lax.broadcasted_iota(jnp.int32, (CHUNK, CHUNK), 1)
    causal = col <= row
    ws = [jnp.where(causal, ws_ref[g], jnp.zeros((CHUNK, CHUNK), BF16)) for g in range(A_GROUPS)]
    s_chunks = []
    for c in range(tm // CHUNK):
        vc = vn[c * CHUNK:(c + 1) * CHUNK, :]
        gd = A_WIDTH // A_GROUPS
        s_c = jnp.concatenate(
            [_dot(ws[g], vc[:, g * gd:(g + 1) * gd]) for g in range(A_GROUPS)], axis=1)
        s_chunks.append(s_c + bs_ref[...])
    gate_b = _gelu(gb_pre)
    q = proj(OFF_Q, C_WIDTH).astype(BF16)
    o_a = (u * jnp.concatenate(s_chunks, axis=0)).astype(BF16)
    ri = jnp.concatenate(
        [_dot(cb[:, h * B_HEAD_DIM:(h + 1) * B_HEAD_DIM], wax_ref[h]) for h in range(B_HEADS)],
        axis=1)
    m_a = proj(OFF_M, D_MODEL)

    r_parts = [ri[:, (2 * h) * B_HEAD_DIM:(2 * h + 1) * B_HEAD_DIM] for h in range(B_HEADS)]
    i_parts = [ri[:, (2 * h + 1) * B_HEAD_DIM:(2 * h + 2) * B_HEAD_DIM] for h in range(B_HEADS)]
    r = _sigmoid_of_twice(jnp.concatenate(r_parts, axis=1) + 0.5 * ba_ref[...])
    i = _sigmoid_of_twice(jnp.concatenate(i_parts, axis=1) + 0.5 * bx_ref[...])
    z = -lam_ref[...]
    softplus = jnp.maximum(z, 0.0) + jnp.log(1.0 + jnp.exp(-jnp.abs(z)))
    log_a = (-LRU_C * softplus) * r
    a = jnp.exp(log_a)
    bx = _sqrt_nonneg(1.0 - a * a) * (i * conv)
    a_steps = [a[SUBLANES * j:SUBLANES * (j + 1), :] for j in range(seg)]
    b_steps = [bx[SUBLANES * j:SUBLANES * (j + 1), :] for j in range(seg)]
    y_a = _dot(o_a, pa_ref[...])
    scores = [_dot(q[:, h * C_HEAD_DIM:(h + 1) * C_HEAD_DIM], kt_ref[h * C_HEAD_DIM:(h + 1) * C_HEAD_DIM, :])
              for h in range(C_HEADS)]

    h_loc, p_tot = b_steps[0], a_steps[0]
    for j in range(1, seg):
        h_loc = a_steps[j] * h_loc + b_steps[j]
        p_tot = p_tot * a_steps[j]
    for d in (1, 2, 4):
        m = sub >= d
        h_sh = jnp.where(m, pltpu.roll(h_loc, d, 0), 0.0)
        p_sh = jnp.where(m, pltpu.roll(p_tot, d, 0), 1.0)
        h_loc = p_tot * h_sh + h_loc
        p_tot = p_tot * p_sh
    h_last = hlast_scr[...]
    h_end = h_loc + p_tot * h_last
    hlast_scr[...] = h_end[SUBLANES - 1:SUBLANES, :]
    h = jnp.where(sub == 0, h_last, pltpu.roll(h_end, 1, 0))
    for j in range(seg):
        h = a_steps[j] * h + b_steps[j]
        for l in range(n_slab):
            seg_scr[l, pl.ds(j, SUBLANES, stride=pitch), :] = h[:, LANES * l:LANES * (l + 1)]
    h_tok = jnp.concatenate(
        [jnp.concatenate([seg_scr[l, pitch * s:pitch * s + seg, :] for l in range(n_slab)], axis=1)
         for s in range(SUBLANES)], axis=0)
    o_b = (h_tok * gate_b).astype(BF16)

    m_b = proj(OFF_M + D_MODEL, D_MODEL)
    probs, denoms = [], []
    for h in range(C_HEADS):
        s = scores[h] * (C_HEAD_DIM ** -0.5)
        p = jnp.exp(s - jnp.max(s, axis=-1, keepdims=True))
        denoms.append(jnp.sum(p, axis=-1, keepdims=True))
        probs.append(p.astype(BF16))
    m_c = proj(OFF_M + 2 * D_MODEL, D_MODEL)
    y_b = _dot(o_b, pb_ref[...])
    o_c = jnp.concatenate(
        [_dot(probs[h], vm_ref[:, h * C_HEAD_DIM:(h + 1) * C_HEAD_DIM]) / denoms[h] for h in range(C_HEADS)],
        axis=1).astype(BF16)
    y_ab = _sigmoid_of_twice(m_a) * y_a + _sigmoid_of_twice(m_b) * y_b
    y_c = _dot(o_c, pc_ref[...])

    y = y_ab + _sigmoid_of_twice(m_c) * y_c
    m = _dot(y.astype(BF16), wo_ref[...]) + bo_ref[...]
    out_ref[...] = _layer_norm(ALPHA * x + m, ln1_g_ref[...], ln1_b_ref[...])


def _resident(block_shape, index_map):
    return pl.BlockSpec(block_shape, index_map, pipeline_mode=pl.Buffered(1))


def _mixer_call(layer, x, lni_g, lni_b, kt, vm, w_in_bf, w):
    bsz, seq, _ = x.shape
    tm = TOKEN_TILE

    def lw(*shape):
        return _resident((None,) + shape, lambda b, s: (layer,) + (0,) * len(shape))

    def vec(*shape):
        return _resident((DEPTH,) + shape, lambda b, s: (0,) * (1 + len(shape)))

    in_specs = [
        pl.BlockSpec((None, tm, D_MODEL), lambda b, s: (b, s, 0)),
        _resident((1, D_MODEL), lambda b, s: (0, 0)),
        _resident((1, D_MODEL), lambda b, s: (0, 0)),
        pl.BlockSpec((None, None, C_WIDTH, N_MEM), lambda b, s: (layer, b, 0, 0)),
        pl.BlockSpec((None, None, N_MEM, C_WIDTH), lambda b, s: (layer, b, 0, 0)),
        _resident((D_MODEL, IN_WIDTH), lambda b, s: (0, 0)), vec(IN_WIDTH),
        vec(A_WIDTH), vec(A_WIDTH),
        lw(A_GROUPS, CHUNK, CHUNK), lw(CHUNK, A_WIDTH),
        vec(CONV_WIDTH, B_WIDTH), vec(B_WIDTH),
        lw(B_HEADS, B_HEAD_DIM, 2 * B_HEAD_DIM), vec(B_WIDTH), vec(B_WIDTH), vec(B_WIDTH),
        lw(A_WIDTH, D_MODEL), lw(B_WIDTH, D_MODEL), lw(C_WIDTH, D_MODEL),
        lw(D_MODEL, D_MODEL), vec(D_MODEL), vec(D_MODEL), vec(D_MODEL),
    ]
    return pl.pallas_call(
        functools.partial(_mixer_kernel, layer),
        grid=(bsz, seq // tm),
        in_specs=in_specs,
        out_specs=pl.BlockSpec((None, tm, D_MODEL), lambda b, s: (b, s, 0)),
        out_shape=jax.ShapeDtypeStruct(x.shape, F32),
        scratch_shapes=[
            pltpu.VMEM((B_WIDTH // LANES, tm + SUBLANES * SUBLANES, LANES), F32),
            pltpu.VMEM(((CONV_WIDTH - 1) * SUBLANES, B_WIDTH), F32),
            pltpu.VMEM((1, B_WIDTH), F32),
        ],
        compiler_params=pltpu.CompilerParams(
            dimension_semantics=("arbitrary", "arbitrary"), vmem_limit_bytes=VMEM_LIMIT),
    )(x, lni_g, lni_b, kt, vm, w_in_bf, w["b_in"], w["ln_v_g"], w["ln_v_b"], w["w_s"], w["b_s"],
      w["conv_w"], w["conv_b"], w["w_ax"], w["b_a"], w["b_x"], w["lam"], w["p_a"], w["p_b"], w["p_c"],
      w["w_o"], w["b_o"], w["ln1_g"], w["ln1_b"])


def _first_max_of4(c):
    m = jnp.maximum(jnp.maximum(c[0], c[1]), jnp.maximum(c[2], c[3]))
    idx = jnp.where(c[0] == m, 0, jnp.where(c[1] == m, 1, jnp.where(c[2] == m, 2, 3)))
    return m, idx


def _moe_kernel(layer, cast_next, x_next_ref, x_ref, wr_ref, br_ref, wup_src_ref, wdn_src_ref, ln2_g_ref, ln2_b_ref,
                *refs):
    step = pl.program_id(0)
    if cast_next:
        w_next_src_ref, out_ref, w_next_ref, wup_scr, wdn_scr, g2_scr, pt_scr, dest_scr, ends_scr, f_scr = refs

        @pl.when(step < IN_WIDTH // CAST_COLS)
        def _():
            scale = jnp.where(step * CAST_COLS >= OFF_M, 0.5, 1.0)
            w_next_ref[...] = (w_next_src_ref[...] * scale).astype(BF16)
    else:
        out_ref, wup_scr, wdn_scr, g2_scr, pt_scr, dest_scr, ends_scr, f_scr = refs

    @pl.when(step < N_EXPERTS)
    def _():
        wup_scr[step] = wup_src_ref[...].astype(BF16)
        grp = lax.div(step, EXPERTS_PER_GROUP)
        row0 = pl.multiple_of(lax.rem(step, EXPERTS_PER_GROUP) * D_EXPERT, D_EXPERT)
        wdn_scr[grp, pl.ds(row0, D_EXPERT), :] = wdn_src_ref[...].astype(BF16)

    @pl.when(step == N_EXPERTS - 1)
    def _():
        _store_route(0, *_route(x_next_ref, wr_ref, br_ref), g2_scr, pt_scr, dest_scr, ends_scr)

    pl.when(step >= N_EXPERTS)(functools.partial(
        _moe_tile, layer, step - (N_EXPERTS - 1), x_next_ref, x_ref, wr_ref, br_ref, wup_scr, wdn_scr,
        ln2_g_ref, ln2_b_ref, out_ref, g2_scr, pt_scr, dest_scr, ends_scr, f_scr))


def _moe_tile(layer, step, x_next_ref, x_ref, wr_ref, br_ref, wup_ref, wdn_ref, ln2_g_ref, ln2_b_ref,
              out_ref, g2_scr, pt_scr, dest_scr, ends_scr, f_scr):
    new = lax.rem(step, 2)
    cur = 1 - new
    ln2_g_ref, ln2_b_ref = ln2_g_ref.at[layer:layer + 1], ln2_b_ref.at[layer:layer + 1]

    new_ends, dest_f, per_token = _route(x_next_ref, wr_ref, br_ref)
    _moe_experts_and_store(new, cur, new_ends, dest_f, per_token, x_ref, wup_ref, wdn_ref, ln2_g_ref, ln2_b_ref,
                           out_ref, g2_scr, pt_scr, dest_scr, ends_scr, f_scr)


def _route(x_next_ref, wr_ref, br_ref):
    tm = x_next_ref.shape[0]
    lgt = lax.dot_general(wr_ref[...], x_next_ref[...].astype(BF16), (((1,), (1,)), ((), ())),
                          preferred_element_type=F32)
    lt = lgt[:ROUTER_LANES, :] + lgt[ROUTER_LANES:, :] + br_ref[...]
    neg = -jnp.inf
    g_logit = [lt[g:g + 1, :] for g in range(N_GROUPS)]
    e_logit = [lt[N_GROUPS + e:N_GROUPS + e + 1, :] for e in range(N_EXPERTS)]

    gmax, g_top = _first_max_of4(g_logit)
    p_top = 1.0 / sum(jnp.exp(gl - gmax) for gl in g_logit)

    cand = [jnp.where(g_top == 0, e_logit[j], jnp.where(
        g_top == 1, e_logit[EXPERTS_PER_GROUP + j], jnp.where(
            g_top == 2, e_logit[2 * EXPERTS_PER_GROUP + j], e_logit[3 * EXPERTS_PER_GROUP + j])))
        for j in range(EXPERTS_PER_GROUP)]
    m1, i1 = _first_max_of4(cand)
    m2, i2 = _first_max_of4([jnp.where(i1 == j, neg, cand[j]) for j in range(EXPERTS_PER_GROUP)])
    e2 = jnp.exp(m2 - m1)
    w1 = p_top / (1.0 + e2)
    w2 = w1 * e2
    gates = [jnp.where(i1 == j, w1, 0.0) + jnp.where(i2 == j, w2, 0.0) for j in range(EXPERTS_PER_GROUP)]

    onehot8 = jnp.concatenate([jnp.where(g_top == g, 1.0, 0.0) for g in range(N_GROUPS)]
                              + [jnp.zeros((SUBLANES - N_GROUPS, tm), F32)], axis=0)
    upto = onehot8
    lane = lax.broadcasted_iota(jnp.int32, (SUBLANES, tm), 1)
    shift = 1
    while shift < tm:
        upto = upto + jnp.where(lane >= shift, pltpu.roll(upto, shift, 1), 0.0)
        shift *= 2
    rank = upto - onehot8
    dest_f = jnp.zeros((1, tm), F32)
    new_ends = []
    end = 0
    for g in range(N_GROUPS):
        count = upto[g:g + 1, tm - 1:tm][0, 0].astype(jnp.int32)
        base = end.astype(F32) if g else 0.0
        dest_f = dest_f + onehot8[g:g + 1, :] * (rank[g:g + 1, :] + base)
        n_blocks = lax.shift_right_logical(count + (ROW_BLOCK - 1), ROW_BLOCK.bit_length() - 1)
        end = end + n_blocks * ROW_BLOCK
        new_ends.append(end)
    per_token = jnp.concatenate(
        [dest_f] + gates + [jnp.zeros((LANES - 1 - EXPERTS_PER_GROUP, tm), F32)], axis=0).T
    return new_ends, dest_f, per_token


def _store_route(slot, new_ends, dest_f, per_token, g2_scr, pt_scr, dest_scr, ends_scr):
    tm = per_token.shape[0]
    for g in range(N_GROUPS):
        ends_scr[slot, g] = new_ends[g]
    dest_scr[slot] = jnp.broadcast_to(dest_f.astype(jnp.int32), (SUBLANES, tm))
    pt_scr[slot] = jnp.where(lax.broadcasted_iota(jnp.int32, (tm, SORTED_ROWS), 1)
                             == per_token[:, 0:1].astype(jnp.int32), 1.0, 0.0).astype(BF16)
    g_hi = per_token.astype(BF16)
    g2_scr[slot] = jnp.concatenate([g_hi, (per_token - g_hi.astype(F32)).astype(BF16)], axis=1)


def _moe_experts_and_store(new, cur, new_ends, dest_f, per_token, x_ref, wup_ref, wdn_ref, ln2_g_ref, ln2_b_ref,
                           out_ref, g2_scr, pt_scr, dest_scr, ends_scr, f_scr):
    tm = x_ref.shape[0]
    ends = [ends_scr[cur, g] for g in range(N_GROUPS)]
    dest_row = dest_scr[cur, 0:1, :]
    block_row = lax.broadcasted_iota(jnp.int32, (ROW_BLOCK, tm), 0)

    def expert_block(b):
        start = b * ROW_BLOCK
        grp = sum((start >= ends[g]).astype(jnp.int32) for g in range(N_GROUPS - 1))
        perm = jnp.where(block_row + start == dest_row, 1.0, 0.0).astype(BF16)
        xsb = _dot(perm, x_ref[...].astype(BF16)).astype(BF16)
        gs2 = _dot(perm, g2_scr[cur])
        gsb = gs2[:, :ROUTER_LANES] + gs2[:, ROUTER_LANES:]
        acts = []
        for j in range(EXPERTS_PER_GROUP):
            h = _dot(xsb, wup_ref[grp * EXPERTS_PER_GROUP + j])
            hg = h[:, :D_EXPERT]
            acts.append((_silu(hg) * h[:, D_EXPERT:] * gsb[:, j + 1:j + 2]).astype(BF16))
        f_scr[start:start + ROW_BLOCK, :] = _dot(jnp.concatenate(acts, axis=1), wdn_ref[grp]).astype(BF16)

    for b in range(ALWAYS_BLOCKS):
        expert_block(b)

    _store_route(new, new_ends, dest_f, per_token, g2_scr, pt_scr, dest_scr, ends_scr)

    for b in range(ALWAYS_BLOCKS, MAX_BLOCKS):
        used = b * ROW_BLOCK < ends[N_GROUPS - 1]
        pl.when(used)(functools.partial(expert_block, b))

        @pl.when(jnp.logical_not(used))
        def _():
            f_scr[b * ROW_BLOCK:(b + 1) * ROW_BLOCK, :] = jnp.zeros((ROW_BLOCK, D_MODEL), BF16)

    for r0 in range(0, tm, FINISH_ROWS):
        rows = slice(r0, r0 + FINISH_ROWS)
        f = _dot(pt_scr[cur, rows, :], f_scr[...])
        out_ref[rows, :] = _layer_norm(ALPHA * x_ref[rows, :] + f, ln2_g_ref[...], ln2_b_ref[...])


def _moe_call(layer, x2d, w, w_in_next=None):
    n_tok = x2d.shape[0]
    tm = MOE_TILE
    cast_next = w_in_next is not None

    def lw(*shape):
        return _resident((None,) + shape, lambda t: (layer,) + (0,) * len(shape))

    n_tiles = n_tok // tm

    def routed(s):
        return jnp.clip(s - (N_EXPERTS - 1), 0, n_tiles - 1)

    def finished(s):
        return jnp.clip(s - N_EXPERTS, 0, n_tiles - 1)

    def expert(s):
        return jnp.minimum(s, N_EXPERTS - 1)

    n_steps = N_EXPERTS + n_tiles
    in_specs = [
        pl.BlockSpec((tm, D_MODEL), lambda s: (routed(s), 0)),
        pl.BlockSpec((tm, D_MODEL), lambda s: (finished(s), 0)),
        lw(2 * ROUTER_LANES, D_MODEL), lw(ROUTER_LANES, 1),
        pl.BlockSpec((None, None, D_MODEL, 2 * D_EXPERT), lambda s: (layer, expert(s), 0, 0)),
        pl.BlockSpec((None, None, D_EXPERT, D_MODEL), lambda s: (layer, expert(s), 0, 0)),
        _resident((DEPTH, D_MODEL), lambda s: (0, 0)), _resident((DEPTH, D_MODEL), lambda s: (0, 0)),
    ]
    out_specs = [pl.BlockSpec((tm, D_MODEL), lambda s: (finished(s), 0))]
    out_shape = [jax.ShapeDtypeStruct(x2d.shape, F32)]
    operands = [x2d, x2d, w["wr"], w["br"], w["w_up"], w["w_down"], w["ln2_g"], w["ln2_b"]]
    if cast_next:
        n_blocks = IN_WIDTH // CAST_COLS
        assert IN_WIDTH % CAST_COLS == 0 and OFF_M % CAST_COLS == 0 and n_blocks <= n_steps
        cast_spec = pl.BlockSpec((D_MODEL, CAST_COLS), lambda s: (0, jnp.minimum(s, n_blocks - 1)))
        in_specs.append(cast_spec)
        out_specs.append(cast_spec)
        out_shape.append(jax.ShapeDtypeStruct((D_MODEL, IN_WIDTH), BF16))
        operands.append(w_in_next)

    outs = pl.pallas_call(
        functools.partial(_moe_kernel, layer, cast_next),
        grid=(n_steps,),
        in_specs=in_specs,
        out_specs=out_specs,
        out_shape=out_shape,
        scratch_shapes=[
            pltpu.VMEM((N_EXPERTS, D_MODEL, 2 * D_EXPERT), BF16),
            pltpu.VMEM((N_GROUPS, EXPERTS_PER_GROUP * D_EXPERT, D_MODEL), BF16),
            pltpu.VMEM((2, tm, 2 * ROUTER_LANES), BF16),
            pltpu.VMEM((2, tm, SORTED_ROWS), BF16),
            pltpu.VMEM((2, SUBLANES, tm), jnp.int32),
            pltpu.SMEM((2, N_GROUPS), jnp.int32),
            pltpu.VMEM((SORTED_ROWS, D_MODEL), BF16),
        ],
        compiler_params=pltpu.CompilerParams(
            dimension_semantics=("arbitrary",), vmem_limit_bytes=VMEM_LIMIT),
    )(*operands)
    return (outs[0], outs[1]) if cast_next else (outs[0], None)


def kernel(x, mem, ln_in_g, ln_in_b, ln_mem_g, ln_mem_b, w_in, b_in, ln_v_g, ln_v_b, w_s, b_s, conv_w, conv_b, w_a, b_a, w_x, b_x, lam, w_kv, p_a, p_b, p_c, w_o, b_o, ln1_g, ln1_b, w_rg, b_rg, w_re, b_re, w_up, w_down, ln2_g, ln2_b):
    bsz, seq, _ = x.shape
    assert seq % TOKEN_TILE == 0 and TOKEN_TILE % CHUNK == 0 and (bsz * seq) % MOE_TILE == 0

    wr = jnp.concatenate(
        [w_rg, w_re, jnp.zeros((DEPTH, D_MODEL, ROUTER_LANES - N_GROUPS - N_EXPERTS), F32)], axis=-1)
    wr_hi = wr.astype(BF16)
    in_scale = jnp.where(jnp.arange(IN_WIDTH) >= OFF_M, 0.5, 1.0).astype(F32)
    w = {
        "b_in": b_in,
        "ln_v_g": ln_v_g, "ln_v_b": ln_v_b,
        "w_s": w_s.astype(BF16),
        "b_s": jnp.repeat(jnp.swapaxes(b_s, 1, 2), A_WIDTH // A_GROUPS, axis=-1),
        "conv_w": conv_w, "conv_b": conv_b,
        "w_ax": (0.5 * jnp.concatenate([w_a, w_x], axis=-1)).astype(BF16),
        "b_a": b_a, "b_x": b_x, "lam": lam,
        "p_a": p_a.astype(BF16), "p_b": p_b.astype(BF16), "p_c": p_c.astype(BF16),
        "w_o": w_o.astype(BF16), "b_o": b_o, "ln1_g": ln1_g, "ln1_b": ln1_b,
        "wr": jnp.swapaxes(jnp.concatenate([wr_hi, (wr - wr_hi.astype(F32)).astype(BF16)], axis=-1), 1, 2),
        "br": jnp.concatenate(
            [b_rg, b_re, jnp.zeros((DEPTH, ROUTER_LANES - N_GROUPS - N_EXPERTS), F32)], axis=-1)[:, :, None],
        "w_up": w_up, "w_down": w_down,
        "ln2_g": ln2_g, "ln2_b": ln2_b,
    }
    lni_g = ln_in_g.reshape(1, D_MODEL)
    lni_b = ln_in_b.reshape(1, D_MODEL)

    kt, vm = _prep_call(mem, ln_mem_g, ln_mem_b, w_kv.astype(BF16))
    w_in_bf = (w_in[0] * in_scale).astype(BF16)
    for layer in range(DEPTH):
        x = _mixer_call(layer, x, lni_g, lni_b, kt, vm, w_in_bf, w)
        x2d, w_in_bf = _moe_call(layer, x.reshape(bsz * seq, D_MODEL), w,
                                 w_in[layer + 1] if layer + 1 < DEPTH else None)
        x = x2d.reshape(bsz, seq, D_MODEL)
    return x
```
